```python
import math
import jax, jax.numpy as jnp
from jax import lax
import numpy as np

D_MODEL = 1024
BATCH = 32
SEQ = 2048
DEPTH = 1

CTX_LEN = 256
GRID_W = 64
SSM_WIDTH = 512
SSM_GROUP = 16
SSM_GROUPS = SSM_WIDTH // SSM_GROUP
SSM_STATE = 64
DT_MIN = 1e-3
DT_MAX = 1e-1
SSM_C_STD = 0.5
SCONV_WIDTH = 512
SCONV_K = 3
FFN_HIDDEN = 2816
FFN_K = 3
N_BRANCH = 2
PROJ_COLS = SSM_WIDTH + 3 * SCONV_WIDTH + N_BRANCH * D_MODEL
N_MOD = 6
EPS = 1e-6

kernel_name = 'hybrid_s5_shortconv_convffn_prefix_block'


def rmsnorm(x, g):
    xf = x.astype(jnp.float32)
    r = lax.rsqrt(jnp.mean(xf * xf, axis=-1, keepdims=True) + EPS)
    return (xf * r).astype(x.dtype) * g


def modulate(h, shift, scale):
    return h * (1 + scale) + shift


def adaln(cond, w, b):
    m = jax.nn.silu(cond) @ w + b
    return jnp.split(m, N_MOD, axis=-1)


def s5_discretise(lam_re, lam_im, log_dt, b_re, b_im):
    lam = lax.complex(lam_re.astype(jnp.float32), lam_im.astype(jnp.float32))
    dt = jnp.exp(log_dt.astype(jnp.float32))[:, None]
    a_bar = jnp.exp(lam * dt)
    b = lax.complex(b_re.astype(jnp.float32), b_im.astype(jnp.float32))
    b_bar = ((a_bar - 1) / lam)[..., None] * b
    return a_bar, b_bar


def _lin_rec_combine(e1, e2):
    a1, b1 = e1
    a2, b2 = e2
    return a1 * a2, a2 * b1 + b2


def s5_states(u, a_bar, b_bar, init, reverse):
    uf = u.astype(jnp.float32)
    if reverse:
        uf = uf[:, ::-1]
    bu = jnp.einsum('blgh,gph->lbgp', uf.astype(jnp.complex64), b_bar)
    if init is not None:
        bu = bu.at[0].add(a_bar * init)
    a = jnp.broadcast_to(a_bar, (bu.shape[0], 1) + a_bar.shape)
    _, states = lax.associative_scan(_lin_rec_combine, (a, bu), axis=0)
    return states


def s5_readout(states, c_re, c_im, reverse):
    c = lax.complex(c_re.astype(jnp.float32), c_im.astype(jnp.float32))
    y = jnp.real(jnp.einsum('lbgp,ghp->blgh', states, c))
    return y[:, ::-1] if reverse else y


def s5_glu(y, w, b):
    z = jax.nn.gelu(y)
    return z * jax.nn.sigmoid(z @ w + b)


def s5_branch(u_lat, u_ctx, lam_re, lam_im, log_dt, b_re, b_im, c_re, c_im, d_skip, glu_w, glu_b, ctx_out):
    bsz, seq, _ = u_lat.shape
    ctx_len = u_ctx.shape[1]
    ul = u_lat.reshape(bsz, seq, SSM_GROUPS, SSM_GROUP)
    uc = u_ctx.reshape(bsz, ctx_len, SSM_GROUPS, SSM_GROUP)
    y_lat = d_skip * u_lat.astype(jnp.float32)
    y_ctx = d_skip * u_ctx.astype(jnp.float32) if ctx_out else None
    for direction in range(2):
        rev = direction == 1
        a_bar, b_bar = s5_discretise(lam_re[direction], lam_im[direction], log_dt[direction],
                                     b_re[direction], b_im[direction])
        s_ctx = s5_states(uc, a_bar, b_bar, None, rev)
        s_lat = s5_states(ul, a_bar, b_bar, s_ctx[-1], rev)
        y_lat = y_lat + s5_readout(s_lat, c_re[direction], c_im[direction], rev).reshape(bsz, seq, SSM_WIDTH)
        if ctx_out:
            y_ctx = y_ctx + s5_readout(s_ctx, c_re[direction], c_im[direction], rev).reshape(bsz, ctx_len, SSM_WIDTH)
    y_lat = s5_glu(y_lat, glu_w, glu_b).astype(u_lat.dtype)
    if ctx_out:
        y_ctx = s5_glu(y_ctx, glu_w, glu_b).astype(u_ctx.dtype)
    return y_lat, y_ctx


def depthwise_conv1d(x, w):
    return lax.conv_general_dilated(x, w[:, None, :], window_strides=(1,), padding='SAME',
                                    dimension_numbers=('NWC', 'WIO', 'NWC'),
                                    feature_group_count=x.shape[-1])


def depthwise_conv2d(x, w, rows, cols):
    bsz, seq, ch = x.shape
    y = lax.conv_general_dilated(x.reshape(bsz, rows, cols, ch), w[:, :, None, :],
                                 window_strides=(1, 1), padding='SAME',
                                 dimension_numbers=('NHWC', 'HWIO', 'NHWC'),
                                 feature_group_count=ch)
    return y.reshape(bsz, seq, ch)


def mixer_merge(p, y_a, sconv_w, proj_a, proj_b, w_out):
    o = SSM_WIDTH
    w = SCONV_WIDTH
    b_gate = p[..., o:o + w]
    c_gate = p[..., o + w:o + 2 * w]
    x_val = p[..., o + 2 * w:o + 3 * w]
    o2 = o + 3 * w
    gate_a = jax.nn.sigmoid(p[..., o2:o2 + D_MODEL])
    gate_b = jax.nn.sigmoid(p[..., o2 + D_MODEL:o2 + 2 * D_MODEL])
    y_b = b_gate * depthwise_conv1d(c_gate * x_val, sconv_w)
    merged = gate_a * (y_a @ proj_a) + gate_b * (y_b @ proj_b)
    return merged @ w_out


def conv_ffn(h, rows, cols, w_up, conv_w, w_down):
    u = depthwise_conv2d(h @ w_up, conv_w, rows, cols)
    a, v = jnp.split(u, 2, axis=-1)
    return (jax.nn.silu(a) * v) @ w_down


def setup_inputs(seed: int = 0) -> dict:
    key = jax.random.key(seed)
    ks = jax.random.split(key, 32)
    f32 = jnp.float32
    G, P, H = SSM_GROUPS, SSM_STATE, SSM_GROUP
    n_idx = jnp.arange(P, dtype=f32)
    lam_re = -0.5 + 0.01 * jax.random.normal(ks[8], (DEPTH, 2, G, P), f32)
    lam_im = math.pi * n_idx + 0.01 * jax.random.normal(ks[9], (DEPTH, 2, G, P), f32)
    log_dt = jax.random.uniform(ks[10], (DEPTH, 2, G), f32, math.log(DT_MIN), math.log(DT_MAX))
    return {
        'x': jax.random.normal(ks[0], (BATCH, SEQ, D_MODEL), f32),
        'c': jax.random.normal(ks[1], (BATCH, D_MODEL), f32),
        'ctx': jax.random.normal(ks[2], (BATCH, CTX_LEN, D_MODEL), f32),
        'c_ctx': jax.random.normal(ks[3], (D_MODEL,), f32),
        'mod_w': jax.random.normal(ks[4], (DEPTH, D_MODEL, N_MOD * D_MODEL), f32) * 0.5 * D_MODEL ** -0.5,
        'mod_b': 0.01 * jax.random.normal(ks[5], (DEPTH, N_MOD * D_MODEL), f32),
        'norm1_g': 1.0 + 0.02 * jax.random.normal(ks[6], (DEPTH, D_MODEL), f32),
        'norm2_g': 1.0 + 0.02 * jax.random.normal(ks[7], (DEPTH, D_MODEL), f32),
        'w_in': jax.random.normal(ks[11], (DEPTH, D_MODEL, PROJ_COLS), f32) * D_MODEL ** -0.5,
        'ssm_lambda_re': lam_re,
        'ssm_lambda_im': lam_im,
        'ssm_log_dt': log_dt,
        'ssm_b_re': jax.random.normal(ks[12], (DEPTH, 2, G, P, H), f32) * (2 * H) ** -0.5,
        'ssm_b_im': jax.random.normal(ks[13], (DEPTH, 2, G, P, H), f32) * (2 * H) ** -0.5,
        'ssm_c_re': jax.random.normal(ks[14], (DEPTH, 2, G, H, P), f32) * SSM_C_STD,
        'ssm_c_im': jax.random.normal(ks[15], (DEPTH, 2, G, H, P), f32) * SSM_C_STD,
        'ssm_d': jax.random.normal(ks[16], (DEPTH, SSM_WIDTH), f32),
        'ssm_glu_w': jax.random.normal(ks[17], (DEPTH, SSM_WIDTH, SSM_WIDTH), f32) * SSM_WIDTH ** -0.5,
        'ssm_glu_b': 0.01 * jax.random.normal(ks[18], (DEPTH, SSM_WIDTH), f32),
        'sconv_w': jax.random.normal(ks[19], (DEPTH, SCONV_K, SCONV_WIDTH), f32) * SCONV_K ** -0.5,
        'proj_a': jax.random.normal(ks[20], (DEPTH, SSM_WIDTH, D_MODEL), f32) * SSM_WIDTH ** -0.5,
        'proj_b': jax.random.normal(ks[21], (DEPTH, SCONV_WIDTH, D_MODEL), f32) * SCONV_WIDTH ** -0.5,
        'w_out': jax.random.normal(ks[22], (DEPTH, D_MODEL, D_MODEL), f32) * D_MODEL ** -0.5,
        'ffn_w_up': jax.random.normal(ks[23], (DEPTH, D_MODEL, 2 * FFN_HIDDEN), f32) * D_MODEL ** -0.5,
        'ffn_conv_w': jax.random.normal(ks[24], (DEPTH, FFN_K, FFN_K, 2 * FFN_HIDDEN), f32) / FFN_K,
        'ffn_w_down': jax.random.normal(ks[25], (DEPTH, FFN_HIDDEN, D_MODEL), f32) * FFN_HIDDEN ** -0.5,
        'final_g': 1.0 + 0.02 * jax.random.normal(ks[26], (D_MODEL,), f32),
    }


def reference(x, c, ctx, c_ctx, mod_w, mod_b, norm1_g, norm2_g, w_in, ssm_lambda_re, ssm_lambda_im,
              ssm_log_dt, ssm_b_re, ssm_b_im, ssm_c_re, ssm_c_im, ssm_d, ssm_glu_w, ssm_glu_b,
              sconv_w, proj_a, proj_b, w_out, ffn_w_up, ffn_conv_w, ffn_w_down, final_g):
    rows = x.shape[1] // GRID_W
    ctx_len = ctx.shape[1]
    for i in range(DEPTH):
        last = i == DEPTH - 1
        sh1, sc1, g1, sh2, sc2, g2 = adaln(c[:, None, :], mod_w[i], mod_b[i])
        csh1, csc1, cg1, csh2, csc2, cg2 = adaln(c_ctx[None, None, :], mod_w[i], mod_b[i])

        h = modulate(rmsnorm(x, norm1_g[i]), sh1, sc1)
        hc = modulate(rmsnorm(ctx, norm1_g[i]), csh1, csc1)
        p = h @ w_in[i]
        pc = hc @ (w_in[i][:, :SSM_WIDTH] if last else w_in[i])
        y_a, y_a_ctx = s5_branch(p[..., :SSM_WIDTH], pc[..., :SSM_WIDTH],
                                 ssm_lambda_re[i], ssm_lambda_im[i], ssm_log_dt[i],
                                 ssm_b_re[i], ssm_b_im[i], ssm_c_re[i], ssm_c_im[i],
                                 ssm_d[i], ssm_glu_w[i], ssm_glu_b[i], not last)
        x = x + g1 * mixer_merge(p, y_a, sconv_w[i], proj_a[i], proj_b[i], w_out[i])
        if not last:
            ctx = ctx + cg1 * mixer_merge(pc, y_a_ctx, sconv_w[i], proj_a[i], proj_b[i], w_out[i])

        h2 = modulate(rmsnorm(x, norm2_g[i]), sh2, sc2)
        x = x + g2 * conv_ffn(h2, rows, GRID_W, ffn_w_up[i], ffn_conv_w[i], ffn_w_down[i])
        if not last:
            hc2 = modulate(rmsnorm(ctx, norm2_g[i]), csh2, csc2)
            ctx = ctx + cg2 * conv_ffn(hc2, 1, ctx_len, ffn_w_up[i], ffn_conv_w[i], ffn_w_down[i])
    return rmsnorm(x, final_g)
```

```python
import functools

import jax
import jax.numpy as jnp
from jax import lax
from jax.experimental import pallas as pl
from jax.experimental.pallas import tpu as pltpu

F32 = jnp.float32
BF16 = jnp.bfloat16

EPS = 1e-6
GRID_W = 64
N_MOD = 6
SSM_CHUNK = 16
BF16_SUBLANES = 16
VMEM_LIMIT_BYTES = 56 * 1024 * 1024
HIGHEST = lax.Precision.HIGHEST


def _params(*semantics):
    return pltpu.CompilerParams(dimension_semantics=semantics, vmem_limit_bytes=VMEM_LIMIT_BYTES)


def _rms(x):
    return lax.rsqrt(jnp.mean(x * x, axis=-1, keepdims=True) + EPS)


def _adaln_kernel(c_ref, w_ref, b_ref, o_ref):
    s = c_ref[...]
    s = s * jax.nn.sigmoid(s)
    o_ref[...] = jnp.dot(s, w_ref[...], preferred_element_type=F32, precision=HIGHEST) + b_ref[...]


def _adaln(cond, w, b):
    r, d = cond.shape
    n = w.shape[1]
    tn = d
    return pl.pallas_call(
        _adaln_kernel,
        grid=(n // tn,),
        in_specs=[pl.BlockSpec((r, d), lambda j: (0, 0)),
                  pl.BlockSpec((d, tn), lambda j: (0, j)),
                  pl.BlockSpec((1, tn), lambda j: (0, j))],
        out_specs=pl.BlockSpec((r, tn), lambda j: (0, j)),
        out_shape=jax.ShapeDtypeStruct((r, n), F32),
        compiler_params=_params("parallel"),
        name="adaln",
    )(cond, w, b.reshape(1, n))


def _norm_mod(x_ref, sh_ref, sc_ref, g_ref):
    x = x_ref[0]
    return ((x * _rms(x)) * g_ref[...] * (1.0 + sc_ref[0]) + sh_ref[0]).astype(BF16)


def _inproj_kernel(x_ref, sh_ref, sc_ref, g_ref, w_ref, u_ref, bg_ref, cx_ref, ga_ref, gb_ref,
                   *, sw, cw, d):
    hb = _norm_mod(x_ref, sh_ref, sc_ref, g_ref)

    def proj(lo, n):
        return jnp.dot(hb, w_ref[:, lo:lo + n], preferred_element_type=F32)

    u_ref[0] = proj(0, sw).astype(BF16)
    bg_ref[0] = proj(sw, cw).astype(BF16)
    cx_ref[0] = (proj(sw + cw, cw) * proj(sw + 2 * cw, cw)).astype(BF16)
    o2 = sw + 3 * cw
    ga_ref[0] = jax.nn.sigmoid(proj(o2, d)).astype(BF16)
    gb_ref[0] = jax.nn.sigmoid(proj(o2 + d, d)).astype(BF16)


def _inproj(x, mods3, g, w, sw, cw, tm):
    bsz, seq, d = x.shape
    cols = w.shape[1]
    tok = lambda n: pl.BlockSpec((1, tm, n), lambda b, i: (b, i, 0))
    mod = lambda k: pl.BlockSpec((1, 1, d), lambda b, i: (b, 0, k))
    out = lambda n: jax.ShapeDtypeStruct((bsz, seq, n), BF16)
    return pl.pallas_call(
        functools.partial(_inproj_kernel, sw=sw, cw=cw, d=d),
        grid=(bsz, seq // tm),
        in_specs=[tok(d), mod(0), mod(1),
                  pl.BlockSpec((1, d), lambda b, i: (0, 0)),
                  pl.BlockSpec((d, cols), lambda b, i: (0, 0))],
        out_specs=[tok(sw), tok(cw), tok(cw), tok(d), tok(d)],
        out_shape=[out(sw), out(cw), out(cw), out(d), out(d)],
        compiler_params=_params("parallel", "parallel"),
        name="inproj",
    )(x, mods3, mods3, g, w)


def _ctxproj_kernel(x_ref, sh_ref, sc_ref, g_ref, w_ref, u_ref):
    hb = _norm_mod(x_ref, sh_ref, sc_ref, g_ref)
    u_ref[0] = jnp.dot(hb, w_ref[...], preferred_element_type=F32).astype(BF16)


def _ctxproj(ctx, mods3, row, g, w, tm):
    bsz, seq, d = ctx.shape
    sw = w.shape[1]
    mod = lambda k: pl.BlockSpec((1, 1, d), lambda b, i: (row, 0, k))
    return pl.pallas_call(
        _ctxproj_kernel,
        grid=(bsz, seq // tm),
        in_specs=[pl.BlockSpec((1, tm, d), lambda b, i: (b, i, 0)), mod(0), mod(1),
                  pl.BlockSpec((1, d), lambda b, i: (0, 0)),
                  pl.BlockSpec((d, sw), lambda b, i: (0, 0))],
        out_specs=pl.BlockSpec((1, tm, sw), lambda b, i: (b, i, 0)),
        out_shape=jax.ShapeDtypeStruct((bsz, seq, sw), BF16),
        compiler_params=_params("parallel", "parallel"),
        name="ctxproj",
    )(ctx, mods3, mods3, g, w)


def _ssm_weights(lam_re, lam_im, log_dt, b_re, b_im, c_re, c_im, d_skip, t_chunk):
    n_dir, n_g, n_p = lam_re.shape
    n_h = b_re.shape[-1]
    dt = jnp.exp(log_dt)[..., None]
    xr, xi = lam_re * dt, lam_im * dt
    mag = jnp.exp(xr)
    a_re, a_im = mag * jnp.cos(xi), mag * jnp.sin(xi)
    n_re = jnp.expm1(xr) * jnp.cos(xi) - 2.0 * jnp.sin(0.5 * xi) ** 2
    n_im = a_im
    den = lam_re * lam_re + lam_im * lam_im
    q_re = (n_re * lam_re + n_im * lam_im) / den
    q_im = (n_im * lam_re - n_re * lam_im) / den
    bb_re = q_re[..., None] * b_re - q_im[..., None] * b_im
    bb_im = q_re[..., None] * b_im + q_im[..., None] * b_re

    pw_re, pw_im = [jnp.ones_like(a_re)], [jnp.zeros_like(a_im)]
    for _ in range(t_chunk):
        pr, pi = pw_re[-1], pw_im[-1]
        pw_re.append(pr * a_re - pi * a_im)
        pw_im.append(pr * a_im + pi * a_re)
    pw_re, pw_im = jnp.stack(pw_re), jnp.stack(pw_im)

    t = jnp.arange(t_chunk)
    e_in = jnp.stack([t_chunk - 1 - t, t], axis=1)
    e_out = jnp.stack([t + 1, t_chunk - t], axis=1)
    dsel = jnp.arange(n_dir)[None, :]

    pin_re, pin_im = pw_re[e_in, dsel], pw_im[e_in, dsel]
    win_re = pin_re[..., None] * bb_re[None] - pin_im[..., None] * bb_im[None]
    win_im = pin_re[..., None] * bb_im[None] + pin_im[..., None] * bb_re[None]

    def to_in(w, k):
        return jnp.transpose(w[:, k], (1, 0, 3, 2)).reshape(n_g, t_chunk * n_h, n_p)

    w_in = jnp.concatenate([to_in(win_re, 0), to_in(win_re, 1), to_in(win_im, 0), to_in(win_im, 1)], axis=-1)

    pout_re, pout_im = pw_re[e_out, dsel], pw_im[e_out, dsel]
    cr, ci = c_re[None], c_im[None]
    e_re = cr * pout_re[:, :, :, None, :] - ci * pout_im[:, :, :, None, :]
    e_im = cr * pout_im[:, :, :, None, :] + ci * pout_re[:, :, :, None, :]

    def to_out(w, k):
        return jnp.transpose(w[:, k], (1, 3, 0, 2)).reshape(n_g, n_p, t_chunk * n_h)

    w_out = jnp.concatenate([to_out(e_re, 0), to_out(e_re, 1), to_out(-e_im, 0), to_out(-e_im, 1)], axis=1)

    g_re = pw_re[:t_chunk, ..., None] * bb_re[None] - pw_im[:t_chunk, ..., None] * bb_im[None]
    g_im = pw_re[:t_chunk, ..., None] * bb_im[None] + pw_im[:t_chunk, ..., None] * bb_re[None]
    kk = (jnp.einsum('dgip,tdgpj->tdgij', c_re, g_re, precision=HIGHEST)
          - jnp.einsum('dgip,tdgpj->tdgij', c_im, g_im, precision=HIGHEST))
    tau = t[None, :] - t[:, None]
    k_f = kk[jnp.clip(tau, 0, None), 0]
    k_b = kk[jnp.clip(-tau, 0, None), 1]
    k_0 = kk[0, 0] + kk[0, 1] + jnp.eye(n_h, dtype=F32)[None] * d_skip.reshape(n_g, n_h, 1)
    tau5 = tau[:, :, None, None, None]
    m = jnp.where(tau5 > 0, k_f, jnp.where(tau5 < 0, k_b, k_0[None, None]))
    m = jnp.transpose(m, (2, 0, 4, 1, 3)).reshape(n_g, t_chunk * n_h, t_chunk * n_h)

    at_re = jnp.concatenate([pw_re[t_chunk, 0], pw_re[t_chunk, 1]], axis=-1)[:, None, :]
    at_im = jnp.concatenate([pw_im[t_chunk, 0], pw_im[t_chunk, 1]], axis=-1)[:, None, :]
    return w_in.astype(BF16), m.astype(BF16), w_out.astype(BF16), at_re, at_im


def _ssm_kernel(u_ref, win_ref, m_ref, wout_ref, are_ref, aim_ref, y_ref, s_ref, xin_ref,
                *, nb, n_ctx, n_lat):
    n_all = n_ctx + n_lat
    half = s_ref.shape[1] // 2
    quarter = half // 2
    s_ref[...] = jnp.dot(u_ref[0], win_ref[0], preferred_element_type=F32)
    a_re = are_ref[0]
    a_im = aim_ref[0]
    is_f = lax.broadcasted_iota(jnp.int32, (nb, half), 1) < quarter

    def step(k, carry):
        x_re, x_im = carry
        cb = jnp.where(k < n_ctx, n_ctx - 1 - k, n_all + n_ctx - 1 - k)
        rf = pl.multiple_of(k * nb, nb)
        rb = pl.multiple_of(cb * nb, nb)
        xin_ref[pl.ds(rf, nb), 0:quarter] = x_re[:, 0:quarter]
        xin_ref[pl.ds(rb, nb), quarter:half] = x_re[:, quarter:half]
        xin_ref[pl.ds(rf, nb), half:half + quarter] = x_im[:, 0:quarter]
        xin_ref[pl.ds(rb, nb), half + quarter:2 * half] = x_im[:, quarter:half]
        s_f = s_ref[pl.ds(rf, nb), :]
        s_b = s_ref[pl.ds(rb, nb), :]
        s_re = jnp.where(is_f, s_f[:, :half], s_b[:, :half])
        s_im = jnp.where(is_f, s_f[:, half:], s_b[:, half:])
        return (a_re * x_re - a_im * x_im + s_re, a_re * x_im + a_im * x_re + s_im)

    zero = jnp.zeros((nb, half), F32)
    lax.fori_loop(0, n_all, step, (zero, zero))

    lat = n_ctx * nb
    y = jnp.dot(u_ref[0, lat:, :], m_ref[0], preferred_element_type=F32)
    y = y + jnp.dot(xin_ref[lat:, :].astype(BF16), wout_ref[0], preferred_element_type=F32)
    y_ref[0] = y.astype(BF16)


def _ssm(u_all, w_in, m, w_out, at_re, at_im, nb, n_ctx, n_lat):
    n_g, rows, th = u_all.shape
    ns = w_in.shape[2]
    per_g = lambda a: pl.BlockSpec((1,) + a.shape[1:], lambda g: (g, 0, 0))
    return pl.pallas_call(
        functools.partial(_ssm_kernel, nb=nb, n_ctx=n_ctx, n_lat=n_lat),
        grid=(n_g,),
        in_specs=[per_g(u_all), per_g(w_in), per_g(m), per_g(w_out), per_g(at_re), per_g(at_im)],
        out_specs=pl.BlockSpec((1, n_lat * nb, th), lambda g: (g, 0, 0)),
        out_shape=jax.ShapeDtypeStruct((n_g, n_lat * nb, th), BF16),
        scratch_shapes=[pltpu.VMEM((rows, ns), F32), pltpu.VMEM((rows, ns), F32)],
        compiler_params=_params("parallel"),
        name="ssm",
    )(u_all, w_in, m, w_out, at_re, at_im)


def _merge_kernel(ya_ref, bg_ref, cx_ref, cxp_ref, cxn_ref, ga_ref, gb_ref, x_ref,
                  g1_ref, sh2_ref, sc2_ref, gluw_ref, glub_ref, scw_ref, pa_ref, pb_ref, wo_ref, n2_ref,
                  x1_ref, h2_ref):
    i = pl.program_id(1)
    tm = cx_ref.shape[1]
    z = jax.nn.gelu(ya_ref[0].astype(F32), approximate=True)
    gate = jax.nn.sigmoid(jnp.dot(z.astype(BF16), gluw_ref[...], preferred_element_type=F32) + glub_ref[...])
    y_a = (z * gate).astype(BF16)

    cx = cx_ref[0].astype(F32)
    prev = cxp_ref[0].astype(F32)[BF16_SUBLANES - 1:BF16_SUBLANES]
    nxt = cxn_ref[0].astype(F32)[0:1]
    prev = jnp.where(i > 0, prev, 0.0)
    nxt = jnp.where(i < pl.num_programs(1) - 1, nxt, 0.0)
    row = lax.broadcasted_iota(jnp.int32, (tm, 1), 0)
    c_m1 = jnp.where(row == 0, prev, pltpu.roll(cx, 1, 0))
    c_p1 = jnp.where(row == tm - 1, nxt, pltpu.roll(cx, tm - 1, 0))
    w = scw_ref[...]
    y_b = (bg_ref[0].astype(F32) * (w[0:1] * c_m1 + w[1:2] * cx + w[2:3] * c_p1)).astype(BF16)

    merged = (ga_ref[0].astype(F32) * jnp.dot(y_a, pa_ref[...], preferred_element_type=F32)
              + gb_ref[0].astype(F32) * jnp.dot(y_b, pb_ref[...], preferred_element_type=F32))
    x1 = x_ref[0] + g1_ref[0] * jnp.dot(merged.astype(BF16), wo_ref[...], preferred_element_type=F32)
    x1_ref[0] = x1
    h2_ref[0] = ((x1 * _rms(x1)) * n2_ref[...] * (1.0 + sc2_ref[0]) + sh2_ref[0]).astype(BF16)


def _merge(ya, bg, cx, ga, gb, x, mods3, glu_w, glu_b, sconv_w, proj_a, proj_b, w_out, n2, tm):
    bsz, seq, d = x.shape
    sw, cw = ya.shape[2], cx.shape[2]
    hb = BF16_SUBLANES
    last_hb = seq // hb - 1
    tok = lambda n: pl.BlockSpec((1, tm, n), lambda b, i: (b, i, 0))
    mod = lambda k: pl.BlockSpec((1, 1, d), lambda b, i: (b, 0, k))
    full = lambda a: pl.BlockSpec(a.shape, lambda b, i: (0,) * a.ndim)
    prev_spec = pl.BlockSpec((1, hb, cw), lambda b, i: (b, jnp.maximum(i * (tm // hb) - 1, 0), 0))
    next_spec = pl.BlockSpec((1, hb, cw), lambda b, i: (b, jnp.minimum((i + 1) * (tm // hb), last_hb), 0))
    consts = (glu_w, glu_b, sconv_w, proj_a, proj_b, w_out, n2)
    return pl.pallas_call(
        _merge_kernel,
        grid=(bsz, seq // tm),
        in_specs=[tok(sw), tok(cw), tok(cw), prev_spec, next_spec, tok(d), tok(d), tok(d),
                  mod(2), mod(3), mod(4)] + [full(a) for a in consts],
        out_specs=[tok(d), tok(d)],
        out_shape=[jax.ShapeDtypeStruct((bsz, seq, d), F32), jax.ShapeDtypeStruct((bsz, seq, d), BF16)],
        compiler_params=_params("parallel", "parallel"),
        name="merge",
    )(ya, bg, cx, cx, cx, ga, gb, x, mods3, mods3, mods3, *consts)


def _ffn_kernel(h_ref, x1_ref, g2_ref, fg_ref, wa_ref, wv_ref, cwa_ref, cwv_ref, wd_ref, o_ref,
                ua_ref, uv_ref, act_ref, *, gw):
    j = pl.program_id(1)
    seq, ck = act_ref.shape
    h = h_ref[0]
    pad = jnp.zeros((gw, ck), F32)
    for u_ref, w_ref in ((ua_ref, wa_ref), (uv_ref, wv_ref)):
        u_ref[0:gw, :] = pad
        u_ref[gw + seq:2 * gw + seq, :] = pad
        u_ref[gw:gw + seq, :] = jnp.dot(h, w_ref[...], preferred_element_type=F32)

    col = lax.broadcasted_iota(jnp.int32, (gw, ck), 0)
    first_col = col == 0
    last_col = col == gw - 1

    def conv(u_ref, cw_ref, base):
        up = u_ref[pl.ds(base, gw), :]
        uc = u_ref[pl.ds(base + gw, gw), :]
        un = u_ref[pl.ds(base + 2 * gw, gw), :]
        w = cw_ref[...]
        c0 = w[0:1] * up + w[3:4] * uc + w[6:7] * un
        c1 = w[1:2] * up + w[4:5] * uc + w[7:8] * un
        c2 = w[2:3] * up + w[5:6] * uc + w[8:9] * un
        return (jnp.where(first_col, 0.0, pltpu.roll(c0, 1, 0)) + c1
                + jnp.where(last_col, 0.0, pltpu.roll(c2, gw - 1, 0)))

    def grid_row(r, carry):
        base = pl.multiple_of(r * gw, gw)
        a = conv(ua_ref, cwa_ref, base)
        v = conv(uv_ref, cwv_ref, base)
        act_ref[pl.ds(base, gw), :] = (a * jax.nn.sigmoid(a) * v).astype(BF16)
        return carry

    lax.fori_loop(0, seq // gw, grid_row, 0)
    contrib = jnp.dot(act_ref[...], wd_ref[...], preferred_element_type=F32)

    @pl.when(j == 0)
    def _():
        o_ref[0] = contrib

    @pl.when(j > 0)
    def _():
        o_ref[0] += contrib

    @pl.when(j == pl.num_programs(1) - 1)
    def _():
        x2 = x1_ref[0] + g2_ref[0] * o_ref[0]
        o_ref[0] = (x2 * _rms(x2)) * fg_ref[...]


def _ffn(h2, x1, mods3, final_g, w_up, conv_w, w_down, ck):
    bsz, seq, d = x1.shape
    hid = w_down.shape[0]
    nj = hid // ck
    seq_spec = lambda: pl.BlockSpec((1, seq, d), lambda b, j: (b, 0, 0))
    return pl.pallas_call(
        functools.partial(_ffn_kernel, gw=GRID_W),
        grid=(bsz, nj),
        in_specs=[pl.BlockSpec((1, seq, d), lambda b, j: (b, 0, 0), pipeline_mode=pl.Buffered(1)),
                  pl.BlockSpec((1, seq, d), lambda b, j: (b, 0, 0), pipeline_mode=pl.Buffered(1)),
                  pl.BlockSpec((1, 1, d), lambda b, j: (b, 0, 5)),
                  pl.BlockSpec((1, d), lambda b, j: (0, 0)),
                  pl.BlockSpec((d, ck), lambda b, j: (0, j)),
                  pl.BlockSpec((d, ck), lambda b, j: (0, nj + j)),
                  pl.BlockSpec((conv_w.shape[0], ck), lambda b, j: (0, j)),
                  pl.BlockSpec((conv_w.shape[0], ck), lambda b, j: (0, nj + j)),
                  pl.BlockSpec((ck, d), lambda b, j: (j, 0))],
        out_specs=seq_spec(),
        out_shape=jax.ShapeDtypeStruct((bsz, seq, d), F32),
        scratch_shapes=[pltpu.VMEM((seq + 2 * GRID_W, ck), F32),
                        pltpu.VMEM((seq + 2 * GRID_W, ck), F32),
                        pltpu.VMEM((seq, ck), BF16)],
        compiler_params=_params("parallel", "arbitrary"),
        name="ffn",
    )(h2, x1, mods3, final_g, w_up, w_up, conv_w, conv_w, w_down)


def _layer(x, ctx, mods3, ctx_row, norm1_g, norm2_g, w_in, ssm, glu_w, glu_b, sconv_w,
           proj_a, proj_b, w_out, ffn_w_up, ffn_conv_w, ffn_w_down, out_g):
    bsz, seq, d = x.shape
    ctx_len = ctx.shape[1]
    n_g, th = ssm[1].shape[0], ssm[1].shape[1]
    sw = glu_w.shape[0]
    cw = sconv_w.shape[1]
    n_h = sw // n_g
    t_chunk = th // n_h
    n_ctx, n_lat = ctx_len // t_chunk, seq // t_chunk
    tm = min(512, seq)
    w_in_b = w_in.astype(BF16)

    u, bg, cx, ga, gb = _inproj(x, mods3, norm1_g, w_in_b, sw, cw, tm)
    u_ctx = _ctxproj(ctx, mods3, ctx_row, norm1_g, w_in_b[:, :sw], min(tm, ctx_len))

    def to_chunks(a, n):
        a = a.reshape(bsz, n, t_chunk, n_g, n_h)
        return jnp.transpose(a, (3, 1, 0, 2, 4)).reshape(n_g, n, bsz, th)

    u_all = jnp.concatenate([to_chunks(u_ctx, n_ctx), to_chunks(u, n_lat)], axis=1)
    y = _ssm(u_all.reshape(n_g, (n_ctx + n_lat) * bsz, th), *ssm, bsz, n_ctx, n_lat)
    y = y.reshape(n_g, n_lat, bsz, t_chunk, n_h)
    ya = jnp.transpose(y, (2, 1, 3, 0, 4)).reshape(bsz, seq, sw)

    x1, h2 = _merge(ya, bg, cx, ga, gb, x, mods3, glu_w.astype(BF16), glu_b.reshape(1, sw), sconv_w,
                    proj_a.astype(BF16), proj_b.astype(BF16), w_out.astype(BF16), norm2_g, tm)
    k2 = ffn_conv_w.shape[0] * ffn_conv_w.shape[1]
    return _ffn(h2, x1, mods3, out_g, ffn_w_up.astype(BF16), ffn_conv_w.reshape(k2, -1),
                ffn_w_down.astype(BF16), 256)


def kernel(x, c, ctx, c_ctx, mod_w, mod_b, norm1_g, norm2_g, w_in, ssm_lambda_re, ssm_lambda_im,
           ssm_log_dt, ssm_b_re, ssm_b_im, ssm_c_re, ssm_c_im, ssm_d, ssm_glu_w, ssm_glu_b,
           sconv_w, proj_a, proj_b, w_out, ffn_w_up, ffn_conv_w, ffn_w_down, final_g):
    depth = mod_w.shape[0]
    assert depth == 1, "context-stream update between layers is not implemented"
    bsz, seq, d = x.shape
    assert seq % GRID_W == 0 and seq % SSM_CHUNK == 0 and ctx.shape[1] % SSM_CHUNK == 0
    i = 0
    rows = -(-(bsz + 1) // 8) * 8
    cond = jnp.zeros((rows, d), F32).at[:bsz].set(c).at[bsz].set(c_ctx)
    mods3 = _adaln(cond, mod_w[i], mod_b[i]).reshape(rows, 1, N_MOD * d)
    ssm = _ssm_weights(ssm_lambda_re[i], ssm_lambda_im[i], ssm_log_dt[i], ssm_b_re[i], ssm_b_im[i],
                       ssm_c_re[i], ssm_c_im[i], ssm_d[i], SSM_CHUNK)
    return _layer(x, ctx, mods3, bsz, norm1_g[i].reshape(1, d), norm2_g[i].reshape(1, d), w_in[i], ssm,
                  ssm_glu_w[i], ssm_glu_b[i], sconv_w[i], proj_a[i], proj_b[i], w_out[i],
                  ffn_w_up[i], ffn_conv_w[i], ffn_w_down[i], final_g.reshape(1, d))
```

```python
import functools

import jax
import jax.numpy as jnp
from jax import lax
from jax.experimental import pallas as pl
from jax.experimental.pallas import tpu as pltpu

F32 = jnp.float32
BF16 = jnp.bfloat16

EPS = 1e-6
GRID_W = 64
N_MOD = 6
SSM_CHUNK = 16
LANES = 128
BF16_SUBLANES = 16
MXU_TILE = 256
VMEM_LIMIT_BYTES = 56 * 1024 * 1024
HIGHEST = lax.Precision.HIGHEST


def _params(*semantics):
    return pltpu.CompilerParams(dimension_semantics=semantics, vmem_limit_bytes=VMEM_LIMIT_BYTES)


def _rms(x):
    return lax.rsqrt(jnp.mean(x * x, axis=-1, keepdims=True) + EPS)


def _adaln_kernel(c_ref, w_ref, b_ref, o_ref):
    s = c_ref[...]
    s = s * jax.nn.sigmoid(s)
    o_ref[...] = jnp.dot(s, w_ref[...], preferred_element_type=F32, precision=HIGHEST) + b_ref[...]


def _adaln(cond, w, b):
    r, d = cond.shape
    n = w.shape[1]
    tn = d
    return pl.pallas_call(
        _adaln_kernel,
        grid=(n // tn,),
        in_specs=[pl.BlockSpec((r, d), lambda j: (0, 0)),
                  pl.BlockSpec((d, tn), lambda j: (0, j)),
                  pl.BlockSpec((1, tn), lambda j: (0, j))],
        out_specs=pl.BlockSpec((r, tn), lambda j: (0, j)),
        out_shape=jax.ShapeDtypeStruct((r, n), F32),
        compiler_params=_params("parallel"),
        name="adaln",
    )(cond, w, b.reshape(1, n))


def _norm_mod(x_ref, sh_ref, sc_ref, g_ref):
    x = x_ref[0]
    return ((x * _rms(x)) * g_ref[...] * (1.0 + sc_ref[0]) + sh_ref[0]).astype(BF16)


def _inproj_kernel(x_ref, sh_ref, sc_ref, g_ref, w_ref, u_ref, bg_ref, cx_ref, ga_ref, gb_ref,
                   *, sw, cw, d):
    hb = _norm_mod(x_ref, sh_ref, sc_ref, g_ref)

    def proj(lo, n):
        return jnp.dot(hb, w_ref[:, lo:lo + n], preferred_element_type=F32)

    u_ref[0] = proj(0, sw).astype(BF16)
    bg_ref[0] = proj(sw, cw).astype(BF16)
    cx_ref[0] = (proj(sw + cw, cw) * proj(sw + 2 * cw, cw)).astype(BF16)
    o2 = sw + 3 * cw
    ga_ref[0] = jax.nn.sigmoid(proj(o2, d)).astype(BF16)
    gb_ref[0] = jax.nn.sigmoid(proj(o2 + d, d)).astype(BF16)


def _inproj(x, mods3, g, w, sw, cw, tm):
    bsz, seq, d = x.shape
    cols = w.shape[1]
    tok = lambda n: pl.BlockSpec((1, tm, n), lambda b, i: (b, i, 0))
    mod = lambda k: pl.BlockSpec((1, 1, d), lambda b, i: (b, 0, k))
    out = lambda n: jax.ShapeDtypeStruct((bsz, seq, n), BF16)
    return pl.pallas_call(
        functools.partial(_inproj_kernel, sw=sw, cw=cw, d=d),
        grid=(bsz, seq // tm),
        in_specs=[tok(d), mod(0), mod(1),
                  pl.BlockSpec((1, d), lambda b, i: (0, 0)),
                  pl.BlockSpec((d, cols), lambda b, i: (0, 0))],
        out_specs=[tok(sw), tok(cw), tok(cw), tok(d), tok(d)],
        out_shape=[out(sw), out(cw), out(cw), out(d), out(d)],
        compiler_params=_params("parallel", "parallel"),
        name="inproj",
    )(x, mods3, mods3, g, w)


def _ctxproj_kernel(x_ref, sh_ref, sc_ref, g_ref, w_ref, u_ref):
    hb = _norm_mod(x_ref, sh_ref, sc_ref, g_ref)
    u_ref[0] = jnp.dot(hb, w_ref[...], preferred_element_type=F32).astype(BF16)


def _ctxproj(ctx, mods3, row, g, w, tm):
    bsz, seq, d = ctx.shape
    sw = w.shape[1]
    mod = lambda k: pl.BlockSpec((1, 1, d), lambda b, i: (row, 0, k))
    return pl.pallas_call(
        _ctxproj_kernel,
        grid=(bsz, seq // tm),
        in_specs=[pl.BlockSpec((1, tm, d), lambda b, i: (b, i, 0)), mod(0), mod(1),
                  pl.BlockSpec((1, d), lambda b, i: (0, 0)),
                  pl.BlockSpec((d, sw), lambda b, i: (0, 0))],
        out_specs=pl.BlockSpec((1, tm, sw), lambda b, i: (b, i, 0)),
        out_shape=jax.ShapeDtypeStruct((bsz, seq, sw), BF16),
        compiler_params=_params("parallel", "parallel"),
        name="ctxproj",
    )(ctx, mods3, mods3, g, w)


def _ssm_weights(lam_re, lam_im, log_dt, b_re, b_im, c_re, c_im, d_skip, t_chunk):
    n_dir, n_g, n_p = lam_re.shape
    n_h = b_re.shape[-1]
    dt = jnp.exp(log_dt)[..., None]
    xr, xi = lam_re * dt, lam_im * dt
    mag = jnp.exp(xr)
    a_re, a_im = mag * jnp.cos(xi), mag * jnp.sin(xi)
    n_re = jnp.expm1(xr) * jnp.cos(xi) - 2.0 * jnp.sin(0.5 * xi) ** 2
    n_im = a_im
    den = lam_re * lam_re + lam_im * lam_im
    q_re = (n_re * lam_re + n_im * lam_im) / den
    q_im = (n_im * lam_re - n_re * lam_im) / den
    bb_re = q_re[..., None] * b_re - q_im[..., None] * b_im
    bb_im = q_re[..., None] * b_im + q_im[..., None] * b_re

    pw_re, pw_im = [jnp.ones_like(a_re)], [jnp.zeros_like(a_im)]
    for _ in range(t_chunk):
        pr, pi = pw_re[-1], pw_im[-1]
        pw_re.append(pr * a_re - pi * a_im)
        pw_im.append(pr * a_im + pi * a_re)
    pw_re, pw_im = jnp.stack(pw_re), jnp.stack(pw_im)

    t = jnp.arange(t_chunk)
    e_in = jnp.stack([t_chunk - 1 - t, t], axis=1)
    e_out = jnp.stack([t + 1, t_chunk - t], axis=1)
    dsel = jnp.arange(n_dir)[None, :]

    pin_re, pin_im = pw_re[e_in, dsel], pw_im[e_in, dsel]
    win_re = pin_re[..., None] * bb_re[None] - pin_im[..., None] * bb_im[None]
    win_im = pin_re[..., None] * bb_im[None] + pin_im[..., None] * bb_re[None]

    def to_in(w, k):
        return jnp.transpose(w[:, k], (1, 0, 3, 2)).reshape(n_g, t_chunk * n_h, n_p)

    w_in = jnp.concatenate([to_in(win_re, 0), to_in(win_re, 1), to_in(win_im, 0), to_in(win_im, 1)], axis=-1)

    pout_re, pout_im = pw_re[e_out, dsel], pw_im[e_out, dsel]
    cr, ci = c_re[None], c_im[None]
    e_re = cr * pout_re[:, :, :, None, :] - ci * pout_im[:, :, :, None, :]
    e_im = cr * pout_im[:, :, :, None, :] + ci * pout_re[:, :, :, None, :]

    def to_out(w, k):
        return jnp.transpose(w[:, k], (1, 3, 0, 2)).reshape(n_g, n_p, t_chunk * n_h)

    w_out = jnp.concatenate([to_out(e_re, 0), to_out(e_re, 1), to_out(-e_im, 0), to_out(-e_im, 1)], axis=1)

    g_re = pw_re[:t_chunk, ..., None] * bb_re[None] - pw_im[:t_chunk, ..., None] * bb_im[None]
    g_im = pw_re[:t_chunk, ..., None] * bb_im[None] + pw_im[:t_chunk, ..., None] * bb_re[None]
    kk = (jnp.einsum('dgip,tdgpj->tdgij', c_re, g_re, precision=HIGHEST)
          - jnp.einsum('dgip,tdgpj->tdgij', c_im, g_im, precision=HIGHEST))
    tau = t[None, :] - t[:, None]
    k_f = kk[jnp.clip(tau, 0, None), 0]
    k_b = kk[jnp.clip(-tau, 0, None), 1]
    k_0 = kk[0, 0] + kk[0, 1] + jnp.eye(n_h, dtype=F32)[None] * d_skip.reshape(n_g, n_h, 1)
    tau5 = tau[:, :, None, None, None]
    m = jnp.where(tau5 > 0, k_f, jnp.where(tau5 < 0, k_b, k_0[None, None]))
    m = jnp.transpose(m, (2, 0, 4, 1, 3)).reshape(n_g, t_chunk * n_h, t_chunk * n_h)

    at_re = jnp.concatenate([pw_re[t_chunk, 0], pw_re[t_chunk, 1]], axis=-1)[:, None, :]
    at_im = jnp.concatenate([pw_im[t_chunk, 0], pw_im[t_chunk, 1]], axis=-1)[:, None, :]
    return w_in.astype(BF16), m.astype(BF16), w_out.astype(BF16), at_re, at_im


def _ssm_kernel(uc_ref, ul_ref, win_ref, m_ref, wout_ref, are_ref, aim_ref, y_ref, s_ref, xin_ref,
                *, nb, n_ctx, n_lat):
    n_all = n_ctx + n_lat
    lat = n_ctx * nb
    half = s_ref.shape[1] // 2
    quarter = half // 2
    s_ref[0:lat, :] = jnp.dot(uc_ref[0], win_ref[0], preferred_element_type=F32)
    s_ref[lat:, :] = jnp.dot(ul_ref[0], win_ref[0], preferred_element_type=F32)
    a_re = are_ref[0]
    a_im = aim_ref[0]
    is_f = lax.broadcasted_iota(jnp.int32, (nb, half), 1) < quarter

    def step(k, carry):
        x_re, x_im = carry
        cb = jnp.where(k < n_ctx, n_ctx - 1 - k, n_all + n_ctx - 1 - k)
        rf = pl.multiple_of(k * nb, nb)
        rb = pl.multiple_of(cb * nb, nb)
        xin_ref[pl.ds(rf, nb), 0:quarter] = x_re[:, 0:quarter]
        xin_ref[pl.ds(rb, nb), quarter:half] = x_re[:, quarter:half]
        xin_ref[pl.ds(rf, nb), half:half + quarter] = x_im[:, 0:quarter]
        xin_ref[pl.ds(rb, nb), half + quarter:2 * half] = x_im[:, quarter:half]
        s_f = s_ref[pl.ds(rf, nb), :]
        s_b = s_ref[pl.ds(rb, nb), :]
        s_re = jnp.where(is_f, s_f[:, :half], s_b[:, :half])
        s_im = jnp.where(is_f, s_f[:, half:], s_b[:, half:])
        return (a_re * x_re - a_im * x_im + s_re, a_re * x_im + a_im * x_re + s_im)

    zero = jnp.zeros((nb, half), F32)
    lax.fori_loop(0, n_all, step, (zero, zero))

    y = jnp.dot(ul_ref[0], m_ref[0], preferred_element_type=F32)
    y = y + jnp.dot(xin_ref[lat:, :].astype(BF16), wout_ref[0], preferred_element_type=F32)
    y_ref[0] = y.astype(BF16)


def _ssm(u_ctx, u_lat, w_in, m, w_out, at_re, at_im, nb):
    n_g, rows_lat, th = u_lat.shape
    rows = u_ctx.shape[1] + rows_lat
    ns = w_in.shape[2]
    per_g = lambda a: pl.BlockSpec((1,) + a.shape[1:], lambda g: (g, 0, 0))
    ins = (u_ctx, u_lat, w_in, m, w_out, at_re, at_im)
    return pl.pallas_call(
        functools.partial(_ssm_kernel, nb=nb, n_ctx=u_ctx.shape[1] // nb, n_lat=rows_lat // nb),
        grid=(n_g,),
        in_specs=[per_g(a) for a in ins],
        out_specs=pl.BlockSpec((1, rows_lat, th), lambda g: (g, 0, 0)),
        out_shape=jax.ShapeDtypeStruct((n_g, rows_lat, th), BF16),
        scratch_shapes=[pltpu.VMEM((rows, ns), F32), pltpu.VMEM((rows, ns), F32)],
        compiler_params=_params("parallel"),
        name="ssm",
    )(*ins)


def _merge_kernel(ya_ref, bg_ref, cx_ref, cxp_ref, cxn_ref, ga_ref, gb_ref, x_ref,
                  g1_ref, sh2_ref, sc2_ref, gluw_ref, glub_ref, scw_ref, pa_ref, pb_ref, wo_ref, n2_ref,
                  x1_ref, h2_ref):
    i = pl.program_id(1)
    tm = cx_ref.shape[1]
    z = jax.nn.gelu(ya_ref[0].astype(F32), approximate=True)
    gate = jax.nn.sigmoid(jnp.dot(z.astype(BF16), gluw_ref[...], preferred_element_type=F32) + glub_ref[...])
    y_a = (z * gate).astype(BF16)

    cx = cx_ref[0].astype(F32)
    prev = cxp_ref[0].astype(F32)[BF16_SUBLANES - 1:BF16_SUBLANES]
    nxt = cxn_ref[0].astype(F32)[0:1]
    prev = jnp.where(i > 0, prev, 0.0)
    nxt = jnp.where(i < pl.num_programs(1) - 1, nxt, 0.0)
    row = lax.broadcasted_iota(jnp.int32, (tm, 1), 0)
    c_m1 = jnp.where(row == 0, prev, pltpu.roll(cx, 1, 0))
    c_p1 = jnp.where(row == tm - 1, nxt, pltpu.roll(cx, tm - 1, 0))
    w = scw_ref[...]
    y_b = (bg_ref[0].astype(F32) * (w[0:1] * c_m1 + w[1:2] * cx + w[2:3] * c_p1)).astype(BF16)

    merged = (ga_ref[0].astype(F32) * jnp.dot(y_a, pa_ref[...], preferred_element_type=F32)
              + gb_ref[0].astype(F32) * jnp.dot(y_b, pb_ref[...], preferred_element_type=F32))
    x1 = x_ref[0] + g1_ref[0] * jnp.dot(merged.astype(BF16), wo_ref[...], preferred_element_type=F32)
    x1_ref[0] = x1
    h2_ref[0] = ((x1 * _rms(x1)) * n2_ref[...] * (1.0 + sc2_ref[0]) + sh2_ref[0]).astype(BF16)


def _merge(ya, bg, cx, ga, gb, x, mods3, glu_w, glu_b, sconv_w, proj_a, proj_b, w_out, n2, tm):
    bsz, seq, d = x.shape
    sw, cw = ya.shape[2], cx.shape[2]
    hb = BF16_SUBLANES
    last_hb = seq // hb - 1
    tok = lambda n: pl.BlockSpec((1, tm, n), lambda b, i: (b, i, 0))
    mod = lambda k: pl.BlockSpec((1, 1, d), lambda b, i: (b, 0, k))
    full = lambda a: pl.BlockSpec(a.shape, lambda b, i: (0,) * a.ndim)
    prev_spec = pl.BlockSpec((1, hb, cw), lambda b, i: (b, jnp.maximum(i * (tm // hb) - 1, 0), 0))
    next_spec = pl.BlockSpec((1, hb, cw), lambda b, i: (b, jnp.minimum((i + 1) * (tm // hb), last_hb), 0))
    consts = (glu_w, glu_b, sconv_w, proj_a, proj_b, w_out, n2)
    return pl.pallas_call(
        _merge_kernel,
        grid=(bsz, seq // tm),
        in_specs=[tok(sw), tok(cw), tok(cw), prev_spec, next_spec, tok(d), tok(d), tok(d),
                  mod(2), mod(3), mod(4)] + [full(a) for a in consts],
        out_specs=[tok(d), tok(d)],
        out_shape=[jax.ShapeDtypeStruct((bsz, seq, d), F32), jax.ShapeDtypeStruct((bsz, seq, d), BF16)],
        compiler_params=_params("parallel", "parallel"),
        name="merge",
    )(ya, bg, cx, cx, cx, ga, gb, x, mods3, mods3, mods3, *consts)


FFN_ROW_BLOCK = 256


def _ffn_kernel(h_ref, x1_ref, g2_ref, fg_ref, wa_ref, wv_ref, cwa_ref, cwv_ref, wd_ref, o_ref,
                upa0_ref, upv0_ref, upa1_ref, upv1_ref, act0_ref, act1_ref, *, gw, nj, n_steps):
    s = pl.program_id(0)
    seq, ck = act0_ref.shape
    d = o_ref.shape[2]
    rb = min(FFN_ROW_BLOCK, seq)
    nt = min(MXU_TILE, d)
    j2 = jnp.clip(s - 2, 0, n_steps - 1) % nj

    @pl.when(s == 0)
    def _():
        for ref in (upa0_ref, upv0_ref, upa1_ref, upv1_ref, act0_ref, act1_ref):
            ref[...] = jnp.zeros(ref.shape, BF16)

    @pl.when(j2 == 0)
    def _():
        o_ref[0] = jnp.zeros(o_ref.shape[1:], F32)

    col = lax.broadcasted_iota(jnp.int32, (gw, ck), 0)
    first_col = col == 0
    last_col = col == gw - 1

    def stages(up_w, up_r, act_w, act_r):
        def up_piece(lo, w_ref, dst):
            def run():
                u = jnp.dot(h_ref[0, lo:lo + rb, :], w_ref[...], preferred_element_type=F32)
                for g0 in range(0, rb, gw):
                    ug = u[g0:g0 + gw, :]
                    r = gw + lo + g0
                    dst[0, r:r + gw, :] = jnp.where(first_col, 0.0, pltpu.roll(ug, 1, 0)).astype(BF16)
                    dst[1, r:r + gw, :] = ug.astype(BF16)
                    dst[2, r:r + gw, :] = jnp.where(last_col, 0.0, pltpu.roll(ug, gw - 1, 0)).astype(BF16)
            return run

        def down_piece(lo, n0):
            def run():
                o_ref[0, lo:lo + rb, n0:n0 + nt] += jnp.dot(act_r[lo:lo + rb, :], wd_ref[:, n0:n0 + nt],
                                                            preferred_element_type=F32)
            return run

        cw = [cwa_ref[...].astype(BF16), cwv_ref[...].astype(BF16)]

        def conv_piece(base, l0):
            def run():
                acc = []
                for src, w in zip(up_r, cw):
                    t = None
                    for dy in range(3):
                        for dx in range(3):
                            k = dy * 3 + dx
                            term = w[k:k + 1, l0:l0 + LANES] * src[dx, base + dy * gw:base + (dy + 1) * gw,
                                                                   l0:l0 + LANES]
                            t = term if t is None else t + term
                    acc.append(t)
                a, v = acc
                act_w[base:base + gw, l0:l0 + LANES] = a * jax.nn.sigmoid(a) * v
            return run

        mxu = []
        for lo in range(0, seq, rb):
            mxu.append((rb * d * ck, up_piece(lo, wa_ref, up_w[0])))
            mxu.append((rb * d * ck, up_piece(lo, wv_ref, up_w[1])))
            mxu += [(rb * ck * nt, down_piece(lo, n0)) for n0 in range(0, d, nt)]
        vpu = [conv_piece(base, l0) for base in range(0, seq, gw) for l0 in range(0, ck, LANES)]
        total = sum(c for c, _ in mxu)
        done, k = 0, 0
        for c, piece in mxu:
            piece()
            done += c
            while k < len(vpu) and (k + 1) * total <= done * len(vpu):
                vpu[k]()
                k += 1
        for piece in vpu[k:]:
            piece()

    @pl.when(s % 2 == 0)
    def _():
        stages((upa0_ref, upv0_ref), (upa1_ref, upv1_ref), act1_ref, act0_ref)

    @pl.when(s % 2 == 1)
    def _():
        stages((upa1_ref, upv1_ref), (upa0_ref, upv0_ref), act0_ref, act1_ref)

    @pl.when(jnp.logical_and(s >= 2, j2 == nj - 1))
    def _():
        x2 = x1_ref[0] + g2_ref[0] * o_ref[0]
        o_ref[0] = (x2 * _rms(x2)) * fg_ref[...]


def _ffn(h2, x1, mods3, final_g, w_up, conv_w, w_down, ck):
    bsz, seq, d = x1.shape
    hid = w_down.shape[0]
    nj = hid // ck
    n_steps = bsz * nj
    taps = conv_w.shape[0]
    at = lambda s, lag: jnp.clip(s - lag, 0, n_steps - 1)
    return pl.pallas_call(
        functools.partial(_ffn_kernel, gw=GRID_W, nj=nj, n_steps=n_steps),
        grid=(n_steps + 2,),
        in_specs=[pl.BlockSpec((1, seq, d), lambda s: (at(s, 0) // nj, 0, 0)),
                  pl.BlockSpec((1, seq, d), lambda s: (at(s, 2) // nj, 0, 0), pipeline_mode=pl.Buffered(1)),
                  pl.BlockSpec((1, 1, d), lambda s: (at(s, 2) // nj, 0, 5)),
                  pl.BlockSpec((1, d), lambda s: (0, 0)),
                  pl.BlockSpec((d, ck), lambda s: (0, at(s, 0) % nj)),
                  pl.BlockSpec((d, ck), lambda s: (0, nj + at(s, 0) % nj)),
                  pl.BlockSpec((taps, ck), lambda s: (0, at(s, 1) % nj)),
                  pl.BlockSpec((taps, ck), lambda s: (0, nj + at(s, 1) % nj)),
                  pl.BlockSpec((ck, d), lambda s: (at(s, 2) % nj, 0))],
        out_specs=pl.BlockSpec((1, seq, d), lambda s: (at(s, 2) // nj, 0, 0)),
        out_shape=jax.ShapeDtypeStruct((bsz, seq, d), F32),
        scratch_shapes=[pltpu.VMEM((3, seq + 2 * GRID_W, ck), BF16)] * 4 + [pltpu.VMEM((seq, ck), BF16)] * 2,
        compiler_params=_params("arbitrary"),
        name="ffn",
    )(h2, x1, mods3, final_g, w_up, w_up, conv_w, conv_w, w_down)


def _layer(x, ctx, mods3, ctx_row, norm1_g, norm2_g, w_in, ssm, glu_w, glu_b, sconv_w,
           proj_a, proj_b, w_out, ffn_w_up, ffn_conv_w, ffn_w_down, out_g):
    bsz, seq, d = x.shape
    ctx_len = ctx.shape[1]
    n_g, th = ssm[1].shape[0], ssm[1].shape[1]
    sw = glu_w.shape[0]
    cw = sconv_w.shape[1]
    n_h = sw // n_g
    t_chunk = th // n_h
    n_ctx, n_lat = ctx_len // t_chunk, seq // t_chunk
    tm = min(512, seq)
    hg = jnp.arange(sw).reshape(n_g, n_h).T.reshape(-1)
    w_in_b = jnp.concatenate([w_in[:, :sw][:, hg], w_in[:, sw:]], axis=1).astype(BF16)

    u, bg, cx, ga, gb = _inproj(x, mods3, norm1_g, w_in_b, sw, cw, tm)
    u_ctx = _ctxproj(ctx, mods3, ctx_row, norm1_g, w_in_b[:, :sw], min(tm, ctx_len))

    def to_chunks(a, n):
        a = a.reshape(bsz, n, t_chunk, n_h, n_g)
        return jnp.transpose(a, (4, 1, 0, 2, 3)).reshape(n_g, n * bsz, th)

    y = _ssm(to_chunks(u_ctx, n_ctx), to_chunks(u, n_lat), *ssm, bsz)
    y = y.reshape(n_g, n_lat, bsz, t_chunk, n_h)
    ya = jnp.transpose(y, (2, 1, 3, 4, 0)).reshape(bsz, seq, sw)

    x1, h2 = _merge(ya, bg, cx, ga, gb, x, mods3, glu_w[hg][:, hg].astype(BF16), glu_b[hg].reshape(1, sw),
                    sconv_w, proj_a[hg].astype(BF16), proj_b.astype(BF16), w_out.astype(BF16), norm2_g, tm)
    k2 = ffn_conv_w.shape[0] * ffn_conv_w.shape[1]
    return _ffn(h2, x1, mods3, out_g, ffn_w_up.astype(BF16), ffn_conv_w.reshape(k2, -1),
                ffn_w_down.astype(BF16), 256)


def kernel(x, c, ctx, c_ctx, mod_w, mod_b, norm1_g, norm2_g, w_in, ssm_lambda_re, ssm_lambda_im,
           ssm_log_dt, ssm_b_re, ssm_b_im, ssm_c_re, ssm_c_im, ssm_d, ssm_glu_w, ssm_glu_b,
           sconv_w, proj_a, proj_b, w_out, ffn_w_up, ffn_conv_w, ffn_w_down, final_g):
    depth = mod_w.shape[0]
    assert depth == 1, "context-stream update between layers is not implemented"
    bsz, seq, d = x.shape
    assert seq % GRID_W == 0 and seq % SSM_CHUNK == 0 and ctx.shape[1] % SSM_CHUNK == 0
    i = 0
    rows = -(-(bsz + 1) // 8) * 8
    cond = jnp.zeros((rows, d), F32).at[:bsz].set(c).at[bsz].set(c_ctx)
    mods3 = _adaln(cond, mod_w[i], mod_b[i]).reshape(rows, 1, N_MOD * d)
    ssm = _ssm_weights(ssm_lambda_re[i], ssm_lambda_im[i], ssm_log_dt[i], ssm_b_re[i], ssm_b_im[i],
                       ssm_c_re[i], ssm_c_im[i], ssm_d[i], SSM_CHUNK)
    return _layer(x, ctx, mods3, bsz, norm1_g[i].reshape(1, d), norm2_g[i].reshape(1, d), w_in[i], ssm,
                  ssm_glu_w[i], ssm_glu_b[i], sconv_w[i], proj_a[i], proj_b[i], w_out[i],
                  ffn_w_up[i], ffn_conv_w[i], ffn_w_down[i], final_g.reshape(1, d))
```

```python
import functools

import jax
import jax.numpy as jnp
from jax import lax
from jax.experimental import pallas as pl
from jax.experimental.pallas import tpu as pltpu

F32 = jnp.float32
BF16 = jnp.bfloat16

EPS = 1e-6
GRID_W = 64
N_MOD = 6
SSM_CHUNK = 16
SUBLANES = 8
LANES = 128
BF16_SUBLANES = 16
MXU_TILE = 256
VMEM_LIMIT_BYTES = 56 * 1024 * 1024
HIGHEST = lax.Precision.HIGHEST


def _params(*semantics):
    return pltpu.CompilerParams(dimension_semantics=semantics, vmem_limit_bytes=VMEM_LIMIT_BYTES)


def _rms(x):
    return lax.rsqrt(jnp.mean(x * x, axis=-1, keepdims=True) + EPS)


def _adaln_kernel(c_ref, w_ref, b_ref, o_ref):
    s = c_ref[...]
    s = s * jax.nn.sigmoid(s)
    o_ref[...] = jnp.dot(s, w_ref[...], preferred_element_type=F32, precision=HIGHEST) + b_ref[...]


def _adaln(cond, w, b):
    r, d = cond.shape
    n = w.shape[1]
    tn = d
    return pl.pallas_call(
        _adaln_kernel,
        grid=(n // tn,),
        in_specs=[pl.BlockSpec((r, d), lambda j: (0, 0)),
                  pl.BlockSpec((d, tn), lambda j: (0, j)),
                  pl.BlockSpec((1, tn), lambda j: (0, j))],
        out_specs=pl.BlockSpec((r, tn), lambda j: (0, j)),
        out_shape=jax.ShapeDtypeStruct((r, n), F32),
        compiler_params=_params("parallel"),
        name="adaln",
    )(cond, w, b.reshape(1, n))


def _norm_mod(x_ref, sh_ref, sc_ref, g_ref):
    x = x_ref[0]
    return ((x * _rms(x)) * g_ref[...] * (1.0 + sc_ref[0]) + sh_ref[0]).astype(BF16)


def _inproj_kernel(x_ref, sh_ref, sc_ref, g_ref, w_ref, u_ref, bg_ref, cx_ref, ga_ref, gb_ref,
                   *, sw, cw, d):
    hb = _norm_mod(x_ref, sh_ref, sc_ref, g_ref)

    def proj(lo, n):
        return jnp.dot(hb, w_ref[:, lo:lo + n], preferred_element_type=F32)

    u_ref[0] = proj(0, sw).astype(BF16)
    bg_ref[0] = proj(sw, cw).astype(BF16)
    cx_ref[0] = (proj(sw + cw, cw) * proj(sw + 2 * cw, cw)).astype(BF16)
    o2 = sw + 3 * cw
    ga_ref[0] = jax.nn.sigmoid(proj(o2, d)).astype(BF16)
    gb_ref[0] = jax.nn.sigmoid(proj(o2 + d, d)).astype(BF16)


def _inproj(x, mods3, g, w, sw, cw, tm):
    bsz, seq, d = x.shape
    cols = w.shape[1]
    tok = lambda n: pl.BlockSpec((1, tm, n), lambda b, i: (b, i, 0))
    mod = lambda k: pl.BlockSpec((1, 1, d), lambda b, i: (b, 0, k))
    out = lambda n: jax.ShapeDtypeStruct((bsz, seq, n), BF16)
    return pl.pallas_call(
        functools.partial(_inproj_kernel, sw=sw, cw=cw, d=d),
        grid=(bsz, seq // tm),
        in_specs=[tok(d), mod(0), mod(1),
                  pl.BlockSpec((1, d), lambda b, i: (0, 0)),
                  pl.BlockSpec((d, cols), lambda b, i: (0, 0))],
        out_specs=[tok(sw), tok(cw), tok(cw), tok(d), tok(d)],
        out_shape=[out(sw), out(cw), out(cw), out(d), out(d)],
        compiler_params=_params("parallel", "parallel"),
        name="inproj",
    )(x, mods3, mods3, g, w)


def _ctxproj_kernel(x_ref, sh_ref, sc_ref, g_ref, w_ref, u_ref):
    hb = _norm_mod(x_ref, sh_ref, sc_ref, g_ref)
    u_ref[0] = jnp.dot(hb, w_ref[...], preferred_element_type=F32).astype(BF16)


def _ctxproj(ctx, mods3, row, g, w, tm):
    bsz, seq, d = ctx.shape
    sw = w.shape[1]
    mod = lambda k: pl.BlockSpec((1, 1, d), lambda b, i: (row, 0, k))
    return pl.pallas_call(
        _ctxproj_kernel,
        grid=(bsz, seq // tm),
        in_specs=[pl.BlockSpec((1, tm, d), lambda b, i: (b, i, 0)), mod(0), mod(1),
                  pl.BlockSpec((1, d), lambda b, i: (0, 0)),
                  pl.BlockSpec((d, sw), lambda b, i: (0, 0))],
        out_specs=pl.BlockSpec((1, tm, sw), lambda b, i: (b, i, 0)),
        out_shape=jax.ShapeDtypeStruct((bsz, seq, sw), BF16),
        compiler_params=_params("parallel", "parallel"),
        name="ctxproj",
    )(ctx, mods3, mods3, g, w)


RELAYOUT_BATCH = BF16_SUBLANES
RELAYOUT_CHUNKS = 2
RELAYOUT_TILES_IN_FLIGHT = 4


def _block_transpose(groups, width):
    n = len(groups[0])
    shape = groups[0][0].shape
    lanes = shape[-1]
    block = lax.broadcasted_iota(jnp.int32, shape, len(shape) - 1) // width
    s = n // 2
    while s:
        keep = (block & s) == 0
        nxt = []
        for regs in groups:
            new = list(regs)
            for i in range(n):
                if i & s == 0:
                    a, b = regs[i], regs[i | s]
                    new[i] = jnp.where(keep, a, pltpu.roll(b, s * width, 1))
                    new[i | s] = jnp.where(keep, pltpu.roll(a, lanes - s * width, 1), b)
            nxt.append(new)
        groups = nxt
        s //= 2
    return groups


def _chunk_tiles(ntok, width, n_h, t_chunk):
    per = LANES // n_h
    n_q = width // LANES
    step = min(RELAYOUT_TILES_IN_FLIGHT, n_q)
    for cl in range(ntok // t_chunk):
        for hf in range(t_chunk // per):
            for q0 in range(0, n_q, step):
                yield cl, hf, range(q0, q0 + step), per, cl * t_chunk + hf * per


def _to_chunks_kernel(u_ref, o_ref, s_ref, *, n_h):
    nb, ntok, width = u_ref.shape
    t_chunk = o_ref.shape[3] // n_h
    pitch = s_ref.shape[1] // nb
    for q in range(width // LANES):
        for b in range(nb):
            s_ref[q, b * pitch:b * pitch + ntok, :] = u_ref[b, :, q * LANES:(q + 1) * LANES].astype(F32)
    for cl, hf, qs, per, tok in _chunk_tiles(ntok, width, n_h, t_chunk):
        groups = [[s_ref[q, pl.ds(tok + j, nb, stride=pitch), :] for j in range(per)] for q in qs]
        for q, regs in zip(qs, _block_transpose(groups, n_h)):
            for gl, r in enumerate(regs):
                o_ref[q * per + gl, cl, :, hf * LANES:(hf + 1) * LANES] = r.astype(BF16)


def _from_chunks_kernel(y_ref, o_ref, s_ref, *, n_h):
    nb, ntok, width = o_ref.shape
    t_chunk = y_ref.shape[3] // n_h
    pitch = s_ref.shape[1] // nb
    for cl, hf, qs, per, tok in _chunk_tiles(ntok, width, n_h, t_chunk):
        groups = [[y_ref[q * per + gl, cl, :, hf * LANES:(hf + 1) * LANES].astype(F32) for gl in range(per)]
                  for q in qs]
        for q, regs in zip(qs, _block_transpose(groups, n_h)):
            for j, r in enumerate(regs):
                s_ref[q, pl.ds(tok + j, nb, stride=pitch), :] = r
    for q in range(width // LANES):
        for b in range(nb):
            o_ref[b, :, q * LANES:(q + 1) * LANES] = s_ref[q, b * pitch:b * pitch + ntok, :].astype(BF16)


def _chunk_specs(bsz, seq, width, n_g):
    n_h = width // n_g
    nb = min(RELAYOUT_BATCH, bsz)
    ntok = RELAYOUT_CHUNKS * SSM_CHUNK
    assert bsz % nb == 0 and seq % ntok == 0 and LANES % n_h == 0 and width % LANES == 0
    nat = pl.BlockSpec((nb, ntok, width), lambda bi, ti: (bi, ti, 0))
    chunked = pl.BlockSpec((n_g, RELAYOUT_CHUNKS, nb, SSM_CHUNK * n_h), lambda bi, ti: (0, ti, bi, 0))
    pitch = ntok + SUBLANES if (ntok // SUBLANES) % 2 == 0 else ntok
    scratch = pltpu.VMEM((width // LANES, nb * pitch, LANES), F32)
    return dict(grid=(bsz // nb, seq // ntok), scratch_shapes=[scratch],
                compiler_params=_params("parallel", "parallel")), nat, chunked, n_h


def _to_chunks(u, n_g):
    bsz, seq, width = u.shape
    common, nat, chunked, n_h = _chunk_specs(bsz, seq, width, n_g)
    out = pl.pallas_call(
        functools.partial(_to_chunks_kernel, n_h=n_h), in_specs=[nat], out_specs=chunked,
        out_shape=jax.ShapeDtypeStruct((n_g, seq // SSM_CHUNK, bsz, SSM_CHUNK * n_h), BF16),
        name="to_chunks", **common)(u)
    return out.reshape(n_g, (seq // SSM_CHUNK) * bsz, SSM_CHUNK * n_h)


def _from_chunks(y, bsz):
    n_g, rows, th = y.shape
    n_h = th // SSM_CHUNK
    seq = rows // bsz * SSM_CHUNK
    common, nat, chunked, n_h = _chunk_specs(bsz, seq, n_g * n_h, n_g)
    return pl.pallas_call(
        functools.partial(_from_chunks_kernel, n_h=n_h), in_specs=[chunked], out_specs=nat,
        out_shape=jax.ShapeDtypeStruct((bsz, seq, n_g * n_h), BF16),
        name="from_chunks", **common)(y.reshape(n_g, seq // SSM_CHUNK, bsz, th))


def _ssm_weights(lam_re, lam_im, log_dt, b_re, b_im, c_re, c_im, d_skip, t_chunk):
    n_dir, n_g, n_p = lam_re.shape
    n_h = b_re.shape[-1]
    dt = jnp.exp(log_dt)[..., None]
    xr, xi = lam_re * dt, lam_im * dt
    mag = jnp.exp(xr)
    a_re, a_im = mag * jnp.cos(xi), mag * jnp.sin(xi)
    n_re = jnp.expm1(xr) * jnp.cos(xi) - 2.0 * jnp.sin(0.5 * xi) ** 2
    n_im = a_im
    den = lam_re * lam_re + lam_im * lam_im
    q_re = (n_re * lam_re + n_im * lam_im) / den
    q_im = (n_im * lam_re - n_re * lam_im) / den
    bb_re = q_re[..., None] * b_re - q_im[..., None] * b_im
    bb_im = q_re[..., None] * b_im + q_im[..., None] * b_re

    pw_re, pw_im = [jnp.ones_like(a_re)], [jnp.zeros_like(a_im)]
    for _ in range(t_chunk):
        pr, pi = pw_re[-1], pw_im[-1]
        pw_re.append(pr * a_re - pi * a_im)
        pw_im.append(pr * a_im + pi * a_re)
    pw_re, pw_im = jnp.stack(pw_re), jnp.stack(pw_im)

    t = jnp.arange(t_chunk)
    e_in = jnp.stack([t_chunk - 1 - t, t], axis=1)
    e_out = jnp.stack([t + 1, t_chunk - t], axis=1)
    dsel = jnp.arange(n_dir)[None, :]

    pin_re, pin_im = pw_re[e_in, dsel], pw_im[e_in, dsel]
    win_re = pin_re[..., None] * bb_re[None] - pin_im[..., None] * bb_im[None]
    win_im = pin_re[..., None] * bb_im[None] + pin_im[..., None] * bb_re[None]

    def to_in(w, k):
        return jnp.transpose(w[:, k], (1, 0, 3, 2)).reshape(n_g, t_chunk * n_h, n_p)

    w_in = jnp.concatenate([to_in(win_re, 0), to_in(win_re, 1), to_in(win_im, 0), to_in(win_im, 1)], axis=-1)

    pout_re, pout_im = pw_re[e_out, dsel], pw_im[e_out, dsel]
    cr, ci = c_re[None], c_im[None]
    e_re = cr * pout_re[:, :, :, None, :] - ci * pout_im[:, :, :, None, :]
    e_im = cr * pout_im[:, :, :, None, :] + ci * pout_re[:, :, :, None, :]

    def to_out(w, k):
        return jnp.transpose(w[:, k], (1, 3, 0, 2)).reshape(n_g, n_p, t_chunk * n_h)

    w_out = jnp.concatenate([to_out(e_re, 0), to_out(e_re, 1), to_out(-e_im, 0), to_out(-e_im, 1)], axis=1)

    g_re = pw_re[:t_chunk, ..., None] * bb_re[None] - pw_im[:t_chunk, ..., None] * bb_im[None]
    g_im = pw_re[:t_chunk, ..., None] * bb_im[None] + pw_im[:t_chunk, ..., None] * bb_re[None]
    kk = (jnp.einsum('dgip,tdgpj->tdgij', c_re, g_re, precision=HIGHEST)
          - jnp.einsum('dgip,tdgpj->tdgij', c_im, g_im, precision=HIGHEST))
    tau = t[None, :] - t[:, None]
    k_f = kk[jnp.clip(tau, 0, None), 0]
    k_b = kk[jnp.clip(-tau, 0, None), 1]
    k_0 = kk[0, 0] + kk[0, 1] + jnp.eye(n_h, dtype=F32)[None] * d_skip.reshape(n_g, n_h, 1)
    tau5 = tau[:, :, None, None, None]
    m = jnp.where(tau5 > 0, k_f, jnp.where(tau5 < 0, k_b, k_0[None, None]))
    m = jnp.transpose(m, (2, 0, 4, 1, 3)).reshape(n_g, t_chunk * n_h, t_chunk * n_h)

    at_re = jnp.concatenate([pw_re[t_chunk, 0], pw_re[t_chunk, 1]], axis=-1)[:, None, :]
    at_im = jnp.concatenate([pw_im[t_chunk, 0], pw_im[t_chunk, 1]], axis=-1)[:, None, :]
    return w_in.astype(BF16), m.astype(BF16), w_out.astype(BF16), at_re, at_im


def _ssm_kernel(uc_ref, ul_ref, win_ref, m_ref, wout_ref, are_ref, aim_ref, y_ref, s_ref, xin_ref,
                *, nb, n_ctx, n_lat):
    n_all = n_ctx + n_lat
    lat = n_ctx * nb
    half = s_ref.shape[1] // 2
    quarter = half // 2
    s_ref[0:lat, :] = jnp.dot(uc_ref[0], win_ref[0], preferred_element_type=F32)
    s_ref[lat:, :] = jnp.dot(ul_ref[0], win_ref[0], preferred_element_type=F32)
    a_re = are_ref[0]
    a_im = aim_ref[0]
    is_f = lax.broadcasted_iota(jnp.int32, (nb, half), 1) < quarter

    def step(k, carry):
        x_re, x_im = carry
        cb = jnp.where(k < n_ctx, n_ctx - 1 - k, n_all + n_ctx - 1 - k)
        rf = pl.multiple_of(k * nb, nb)
        rb = pl.multiple_of(cb * nb, nb)
        xin_ref[pl.ds(rf, nb), 0:quarter] = x_re[:, 0:quarter]
        xin_ref[pl.ds(rb, nb), quarter:half] = x_re[:, quarter:half]
        xin_ref[pl.ds(rf, nb), half:half + quarter] = x_im[:, 0:quarter]
        xin_ref[pl.ds(rb, nb), half + quarter:2 * half] = x_im[:, quarter:half]
        s_f = s_ref[pl.ds(rf, nb), :]
        s_b = s_ref[pl.ds(rb, nb), :]
        s_re = jnp.where(is_f, s_f[:, :half], s_b[:, :half])
        s_im = jnp.where(is_f, s_f[:, half:], s_b[:, half:])
        return (a_re * x_re - a_im * x_im + s_re, a_re * x_im + a_im * x_re + s_im)

    zero = jnp.zeros((nb, half), F32)
    lax.fori_loop(0, n_all, step, (zero, zero))

    y = jnp.dot(ul_ref[0], m_ref[0], preferred_element_type=F32)
    y = y + jnp.dot(xin_ref[lat:, :].astype(BF16), wout_ref[0], preferred_element_type=F32)
    y_ref[0] = y.astype(BF16)


def _ssm(u_ctx, u_lat, w_in, m, w_out, at_re, at_im, nb):
    n_g, rows_lat, th = u_lat.shape
    rows = u_ctx.shape[1] + rows_lat
    ns = w_in.shape[2]
    per_g = lambda a: pl.BlockSpec((1,) + a.shape[1:], lambda g: (g, 0, 0))
    ins = (u_ctx, u_lat, w_in, m, w_out, at_re, at_im)
    return pl.pallas_call(
        functools.partial(_ssm_kernel, nb=nb, n_ctx=u_ctx.shape[1] // nb, n_lat=rows_lat // nb),
        grid=(n_g,),
        in_specs=[per_g(a) for a in ins],
        out_specs=pl.BlockSpec((1, rows_lat, th), lambda g: (g, 0, 0)),
        out_shape=jax.ShapeDtypeStruct((n_g, rows_lat, th), BF16),
        scratch_shapes=[pltpu.VMEM((rows, ns), F32), pltpu.VMEM((rows, ns), F32)],
        compiler_params=_params("parallel"),
        name="ssm",
    )(*ins)


def _merge_kernel(ya_ref, bg_ref, cx_ref, cxp_ref, cxn_ref, ga_ref, gb_ref, x_ref,
                  g1_ref, sh2_ref, sc2_ref, gluw_ref, glub_ref, scw_ref, pa_ref, pb_ref, wo_ref, n2_ref,
                  x1_ref, h2_ref):
    i = pl.program_id(1)
    tm = cx_ref.shape[1]
    z = jax.nn.gelu(ya_ref[0].astype(F32), approximate=True)
    gate = jax.nn.sigmoid(jnp.dot(z.astype(BF16), gluw_ref[...], preferred_element_type=F32) + glub_ref[...])
    y_a = (z * gate).astype(BF16)

    cx = cx_ref[0].astype(F32)
    prev = cxp_ref[0].astype(F32)[BF16_SUBLANES - 1:BF16_SUBLANES]
    nxt = cxn_ref[0].astype(F32)[0:1]
    prev = jnp.where(i > 0, prev, 0.0)
    nxt = jnp.where(i < pl.num_programs(1) - 1, nxt, 0.0)
    row = lax.broadcasted_iota(jnp.int32, (tm, 1), 0)
    c_m1 = jnp.where(row == 0, prev, pltpu.roll(cx, 1, 0))
    c_p1 = jnp.where(row == tm - 1, nxt, pltpu.roll(cx, tm - 1, 0))
    w = scw_ref[...]
    y_b = (bg_ref[0].astype(F32) * (w[0:1] * c_m1 + w[1:2] * cx + w[2:3] * c_p1)).astype(BF16)

    merged = (ga_ref[0].astype(F32) * jnp.dot(y_a, pa_ref[...], preferred_element_type=F32)
              + gb_ref[0].astype(F32) * jnp.dot(y_b, pb_ref[...], preferred_element_type=F32))
    x1 = x_ref[0] + g1_ref[0] * jnp.dot(merged.astype(BF16), wo_ref[...], preferred_element_type=F32)
    x1_ref[0] = x1
    h2_ref[0] = ((x1 * _rms(x1)) * n2_ref[...] * (1.0 + sc2_ref[0]) + sh2_ref[0]).astype(BF16)


def _merge(ya, bg, cx, ga, gb, x, mods3, glu_w, glu_b, sconv_w, proj_a, proj_b, w_out, n2, tm):
    bsz, seq, d = x.shape
    sw, cw = ya.shape[2], cx.shape[2]
    hb = BF16_SUBLANES
    last_hb = seq // hb - 1
    tok = lambda n: pl.BlockSpec((1, tm, n), lambda b, i: (b, i, 0))
    mod = lambda k: pl.BlockSpec((1, 1, d), lambda b, i: (b, 0, k))
    full = lambda a: pl.BlockSpec(a.shape, lambda b, i: (0,) * a.ndim)
    prev_spec = pl.BlockSpec((1, hb, cw), lambda b, i: (b, jnp.maximum(i * (tm // hb) - 1, 0), 0))
    next_spec = pl.BlockSpec((1, hb, cw), lambda b, i: (b, jnp.minimum((i + 1) * (tm // hb), last_hb), 0))
    consts = (glu_w, glu_b, sconv_w, proj_a, proj_b, w_out, n2)
    return pl.pallas_call(
        _merge_kernel,
        grid=(bsz, seq // tm),
        in_specs=[tok(sw), tok(cw), tok(cw), prev_spec, next_spec, tok(d), tok(d), tok(d),
                  mod(2), mod(3), mod(4)] + [full(a) for a in consts],
        out_specs=[tok(d), tok(d)],
        out_shape=[jax.ShapeDtypeStruct((bsz, seq, d), F32), jax.ShapeDtypeStruct((bsz, seq, d), BF16)],
        compiler_params=_params("parallel", "parallel"),
        name="merge",
    )(ya, bg, cx, cx, cx, ga, gb, x, mods3, mods3, mods3, *consts)


FFN_ROW_BLOCK = 512


def _ffn_kernel(h_ref, x1_ref, g2_ref, fg_ref, wa_ref, wv_ref, cwa_ref, cwv_ref, wd_ref, o_ref,
                upa0_ref, upv0_ref, upa1_ref, upv1_ref, act0_ref, act1_ref, *, gw, nj, n_steps):
    s = pl.program_id(0)
    seq, ck = act0_ref.shape
    d = o_ref.shape[2]
    rb = min(FFN_ROW_BLOCK, seq)
    nt = min(MXU_TILE, d)
    j2 = jnp.clip(s - 2, 0, n_steps - 1) % nj

    @pl.when(s == 0)
    def _():
        for ref in (upa0_ref, upv0_ref, upa1_ref, upv1_ref, act0_ref, act1_ref):
            ref[...] = jnp.zeros(ref.shape, BF16)

    @pl.when(j2 == 0)
    def _():
        o_ref[0] = jnp.zeros(o_ref.shape[1:], F32)

    col = lax.broadcasted_iota(jnp.int32, (gw, ck), 0)
    first_col = col == 0
    last_col = col == gw - 1

    def stages(up_w, up_r, act_w, act_r):
        def up_piece(lo, w_ref, dst):
            def run():
                u = jnp.dot(h_ref[0, lo:lo + rb, :], w_ref[...], preferred_element_type=F32)
                for g0 in range(0, rb, gw):
                    ug = u[g0:g0 + gw, :]
                    r = gw + lo + g0
                    dst[0, r:r + gw, :] = jnp.where(first_col, 0.0, pltpu.roll(ug, 1, 0)).astype(BF16)
                    dst[1, r:r + gw, :] = ug.astype(BF16)
                    dst[2, r:r + gw, :] = jnp.where(last_col, 0.0, pltpu.roll(ug, gw - 1, 0)).astype(BF16)
            return run

        def down_piece(lo, n0):
            def run():
                o_ref[0, lo:lo + rb, n0:n0 + nt] += jnp.dot(act_r[lo:lo + rb, :], wd_ref[:, n0:n0 + nt],
                                                            preferred_element_type=F32)
            return run

        cw = [cwa_ref[...].astype(BF16), cwv_ref[...].astype(BF16)]

        def conv_piece(base, l0):
            def run():
                acc = []
                for src, w in zip(up_r, cw):
                    t = None
                    for dy in range(3):
                        for dx in range(3):
                            k = dy * 3 + dx
                            term = w[k:k + 1, l0:l0 + LANES] * src[dx, base + dy * gw:base + (dy + 1) * gw,
                                                                   l0:l0 + LANES]
                            t = term if t is None else t + term
                    acc.append(t)
                a, v = acc
                act_w[base:base + gw, l0:l0 + LANES] = a * jax.nn.sigmoid(a) * v
            return run

        mxu = []
        for lo in range(0, seq, rb):
            mxu.append((rb * d * ck, up_piece(lo, wa_ref, up_w[0])))
            mxu.append((rb * d * ck, up_piece(lo, wv_ref, up_w[1])))
            mxu += [(rb * ck * nt, down_piece(lo, n0)) for n0 in range(0, d, nt)]
        vpu = [conv_piece(base, l0) for base in range(0, seq, gw) for l0 in range(0, ck, LANES)]
        total = sum(c for c, _ in mxu)
        done, k = 0, 0
        for c, piece in mxu:
            piece()
            done += c
            while k < len(vpu) and (k + 1) * total <= done * len(vpu):
                vpu[k]()
                k += 1
        for piece in vpu[k:]:
            piece()

    @pl.when(s % 2 == 0)
    def _():
        stages((upa0_ref, upv0_ref), (upa1_ref, upv1_ref), act1_ref, act0_ref)

    @pl.when(s % 2 == 1)
    def _():
        stages((upa1_ref, upv1_ref), (upa0_ref, upv0_ref), act0_ref, act1_ref)

    @pl.when(jnp.logical_and(s >= 2, j2 == nj - 1))
    def _():
        x2 = x1_ref[0] + g2_ref[0] * o_ref[0]
        o_ref[0] = (x2 * _rms(x2)) * fg_ref[...]


def _ffn(h2, x1, mods3, final_g, w_up, conv_w, w_down, ck):
    bsz, seq, d = x1.shape
    hid = w_down.shape[0]
    nj = hid // ck
    n_steps = bsz * nj
    taps = conv_w.shape[0]
    at = lambda s, lag: jnp.clip(s - lag, 0, n_steps - 1)
    return pl.pallas_call(
        functools.partial(_ffn_kernel, gw=GRID_W, nj=nj, n_steps=n_steps),
        grid=(n_steps + 2,),
        in_specs=[pl.BlockSpec((1, seq, d), lambda s: (at(s, 0) // nj, 0, 0)),
                  pl.BlockSpec((1, seq, d), lambda s: (at(s, 2) // nj, 0, 0), pipeline_mode=pl.Buffered(1)),
                  pl.BlockSpec((1, 1, d), lambda s: (at(s, 2) // nj, 0, 5)),
                  pl.BlockSpec((1, d), lambda s: (0, 0)),
                  pl.BlockSpec((d, ck), lambda s: (0, at(s, 0) % nj)),
                  pl.BlockSpec((d, ck), lambda s: (0, nj + at(s, 0) % nj)),
                  pl.BlockSpec((taps, ck), lambda s: (0, at(s, 1) % nj)),
                  pl.BlockSpec((taps, ck), lambda s: (0, nj + at(s, 1) % nj)),
                  pl.BlockSpec((ck, d), lambda s: (at(s, 2) % nj, 0))],
        out_specs=pl.BlockSpec((1, seq, d), lambda s: (at(s, 2) // nj, 0, 0)),
        out_shape=jax.ShapeDtypeStruct((bsz, seq, d), F32),
        scratch_shapes=[pltpu.VMEM((3, seq + 2 * GRID_W, ck), BF16)] * 4 + [pltpu.VMEM((seq, ck), BF16)] * 2,
        compiler_params=_params("arbitrary"),
        name="ffn",
    )(h2, x1, mods3, final_g, w_up, w_up, conv_w, conv_w, w_down)


def _layer(x, ctx, mods3, ctx_row, norm1_g, norm2_g, w_in, ssm, glu_w, glu_b, sconv_w,
           proj_a, proj_b, w_out, ffn_w_up, ffn_conv_w, ffn_w_down, out_g):
    bsz, seq, d = x.shape
    ctx_len = ctx.shape[1]
    n_g, th = ssm[1].shape[0], ssm[1].shape[1]
    sw = glu_w.shape[0]
    cw = sconv_w.shape[1]
    n_h = sw // n_g
    t_chunk = th // n_h
    n_ctx, n_lat = ctx_len // t_chunk, seq // t_chunk
    tm = min(512, seq)
    w_in_b = w_in.astype(BF16)

    u, bg, cx, ga, gb = _inproj(x, mods3, norm1_g, w_in_b, sw, cw, tm)
    u_ctx = _ctxproj(ctx, mods3, ctx_row, norm1_g, w_in_b[:, :sw], min(tm, ctx_len))
    y = _ssm(_to_chunks(u_ctx, n_g), _to_chunks(u, n_g), *ssm, bsz)
    ya = _from_chunks(y, bsz)

    x1, h2 = _merge(ya, bg, cx, ga, gb, x, mods3, glu_w.astype(BF16), glu_b.reshape(1, sw), sconv_w,
                    proj_a.astype(BF16), proj_b.astype(BF16), w_out.astype(BF16), norm2_g, tm)
    k2 = ffn_conv_w.shape[0] * ffn_conv_w.shape[1]
    return _ffn(h2, x1, mods3, out_g, ffn_w_up.astype(BF16), ffn_conv_w.reshape(k2, -1),
                ffn_w_down.astype(BF16), 256)


def kernel(x, c, ctx, c_ctx, mod_w, mod_b, norm1_g, norm2_g, w_in, ssm_lambda_re, ssm_lambda_im,
           ssm_log_dt, ssm_b_re, ssm_b_im, ssm_c_re, ssm_c_im, ssm_d, ssm_glu_w, ssm_glu_b,
           sconv_w, proj_a, proj_b, w_out, ffn_w_up, ffn_conv_w, ffn_w_down, final_g):
    depth = mod_w.shape[0]
    assert depth == 1, "context-stream update between layers is not implemented"
    bsz, seq, d = x.shape
    assert seq % GRID_W == 0 and seq % SSM_CHUNK == 0 and ctx.shape[1] % SSM_CHUNK == 0
    i = 0
    rows = -(-(bsz + 1) // 8) * 8
    cond = jnp.zeros((rows, d), F32).at[:bsz].set(c).at[bsz].set(c_ctx)
    mods3 = _adaln(cond, mod_w[i], mod_b[i]).reshape(rows, 1, N_MOD * d)
    ssm = _ssm_weights(ssm_lambda_re[i], ssm_lambda_im[i], ssm_log_dt[i], ssm_b_re[i], ssm_b_im[i],
                       ssm_c_re[i], ssm_c_im[i], ssm_d[i], SSM_CHUNK)
    return _layer(x, ctx, mods3, bsz, norm1_g[i].reshape(1, d), norm2_g[i].reshape(1, d), w_in[i], ssm,
                  ssm_glu_w[i], ssm_glu_b[i], sconv_w[i], proj_a[i], proj_b[i], w_out[i],
                  ffn_w_up[i], ffn_conv_w[i], ffn_w_down[i], final_g.reshape(1, d))
```

```python
import functools

import jax
import jax.numpy as jnp
from jax import lax
from jax.experimental import pallas as pl
from jax.experimental.pallas import tpu as pltpu

F32 = jnp.float32
BF16 = jnp.bfloat16

EPS = 1e-6
GRID_W = 64
N_MOD = 6
SSM_CHUNK = 16
SUBLANES = 8
LANES = 128
BF16_SUBLANES = 16
MXU_TILE = 256
VMEM_LIMIT_BYTES = 56 * 1024 * 1024
HIGHEST = lax.Precision.HIGHEST


def _params(*semantics):
    return pltpu.CompilerParams(dimension_semantics=semantics, vmem_limit_bytes=VMEM_LIMIT_BYTES)


def _rms(x):
    return lax.rsqrt(jnp.mean(x * x, axis=-1, keepdims=True) + EPS)


def _adaln_kernel(c_ref, w_ref, b_ref, o_ref):
    s = c_ref[...]
    s = s * jax.nn.sigmoid(s)
    o_ref[...] = jnp.dot(s, w_ref[...], preferred_element_type=F32, precision=HIGHEST) + b_ref[...]


def _adaln(cond, w, b):
    r, d = cond.shape
    n = w.shape[1]
    tn = d
    return pl.pallas_call(
        _adaln_kernel,
        grid=(n // tn,),
        in_specs=[pl.BlockSpec((r, d), lambda j: (0, 0)),
                  pl.BlockSpec((d, tn), lambda j: (0, j)),
                  pl.BlockSpec((1, tn), lambda j: (0, j))],
        out_specs=pl.BlockSpec((r, tn), lambda j: (0, j)),
        out_shape=jax.ShapeDtypeStruct((r, n), F32),
        compiler_params=_params("parallel"),
        name="adaln",
    )(cond, w, b.reshape(1, n))


def _norm_mod(x_ref, sh_ref, sc_ref, g_ref):
    x = x_ref[0]
    return ((x * _rms(x)) * g_ref[...] * (1.0 + sc_ref[0]) + sh_ref[0]).astype(BF16)


def _inproj_kernel(x_ref, sh_ref, sc_ref, g_ref, w_ref, u_ref, bg_ref, cx_ref, ga_ref, gb_ref,
                   *, sw, cw, d):
    hb = _norm_mod(x_ref, sh_ref, sc_ref, g_ref)

    def proj(lo, n):
        return jnp.dot(hb, w_ref[:, lo:lo + n], preferred_element_type=F32)

    u_ref[0] = proj(0, sw).astype(BF16)
    bg_ref[0] = proj(sw, cw).astype(BF16)
    cx_ref[0] = (proj(sw + cw, cw) * proj(sw + 2 * cw, cw)).astype(BF16)
    o2 = sw + 3 * cw
    ga_ref[0] = jax.nn.sigmoid(proj(o2, d)).astype(BF16)
    gb_ref[0] = jax.nn.sigmoid(proj(o2 + d, d)).astype(BF16)


def _inproj(x, mods3, g, w, sw, cw, tm):
    bsz, seq, d = x.shape
    cols = w.shape[1]
    tok = lambda n: pl.BlockSpec((1, tm, n), lambda b, i: (b, i, 0))
    mod = lambda k: pl.BlockSpec((1, 1, d), lambda b, i: (b, 0, k))
    out = lambda n: jax.ShapeDtypeStruct((bsz, seq, n), BF16)
    return pl.pallas_call(
        functools.partial(_inproj_kernel, sw=sw, cw=cw, d=d),
        grid=(bsz, seq // tm),
        in_specs=[tok(d), mod(0), mod(1),
                  pl.BlockSpec((1, d), lambda b, i: (0, 0)),
                  pl.BlockSpec((d, cols), lambda b, i: (0, 0))],
        out_specs=[tok(sw), tok(cw), tok(cw), tok(d), tok(d)],
        out_shape=[out(sw), out(cw), out(cw), out(d), out(d)],
        compiler_params=_params("parallel", "parallel"),
        name="inproj",
    )(x, mods3, mods3, g, w)


def _ctxproj_kernel(x_ref, sh_ref, sc_ref, g_ref, w_ref, u_ref):
    hb = _norm_mod(x_ref, sh_ref, sc_ref, g_ref)
    u_ref[0] = jnp.dot(hb, w_ref[...], preferred_element_type=F32).astype(BF16)


def _ctxproj(ctx, mods3, row, g, w, tm):
    bsz, seq, d = ctx.shape
    sw = w.shape[1]
    mod = lambda k: pl.BlockSpec((1, 1, d), lambda b, i: (row, 0, k))
    return pl.pallas_call(
        _ctxproj_kernel,
        grid=(bsz, seq // tm),
        in_specs=[pl.BlockSpec((1, tm, d), lambda b, i: (b, i, 0)), mod(0), mod(1),
                  pl.BlockSpec((1, d), lambda b, i: (0, 0)),
                  pl.BlockSpec((d, sw), lambda b, i: (0, 0))],
        out_specs=pl.BlockSpec((1, tm, sw), lambda b, i: (b, i, 0)),
        out_shape=jax.ShapeDtypeStruct((bsz, seq, sw), BF16),
        compiler_params=_params("parallel", "parallel"),
        name="ctxproj",
    )(ctx, mods3, mods3, g, w)


RELAYOUT_BATCH = BF16_SUBLANES
RELAYOUT_CHUNKS = 2
RELAYOUT_TILES_IN_FLIGHT = 4


def _block_transpose(groups, width):
    n = len(groups[0])
    shape = groups[0][0].shape
    lanes = shape[-1]
    block = lax.broadcasted_iota(jnp.int32, shape, len(shape) - 1) // width
    s = n // 2
    while s:
        keep = (block & s) == 0
        nxt = []
        for regs in groups:
            new = list(regs)
            for i in range(n):
                if i & s == 0:
                    a, b = regs[i], regs[i | s]
                    new[i] = jnp.where(keep, a, pltpu.roll(b, s * width, 1))
                    new[i | s] = jnp.where(keep, pltpu.roll(a, lanes - s * width, 1), b)
            nxt.append(new)
        groups = nxt
        s //= 2
    return groups


def _chunk_tiles(ntok, width, n_h, t_chunk):
    per = LANES // n_h
    n_q = width // LANES
    step = min(RELAYOUT_TILES_IN_FLIGHT, n_q)
    for cl in range(ntok // t_chunk):
        for hf in range(t_chunk // per):
            for q0 in range(0, n_q, step):
                yield cl, hf, range(q0, q0 + step), per, cl * t_chunk + hf * per


def _to_chunks_kernel(u_ref, o_ref, s_ref, *, n_h):
    nb, ntok, width = u_ref.shape
    t_chunk = o_ref.shape[3] // n_h
    pitch = s_ref.shape[1] // nb
    for q in range(width // LANES):
        for b in range(nb):
            s_ref[q, b * pitch:b * pitch + ntok, :] = u_ref[b, :, q * LANES:(q + 1) * LANES].astype(F32)
    for cl, hf, qs, per, tok in _chunk_tiles(ntok, width, n_h, t_chunk):
        groups = [[s_ref[q, pl.ds(tok + j, nb, stride=pitch), :] for j in range(per)] for q in qs]
        for q, regs in zip(qs, _block_transpose(groups, n_h)):
            for gl, r in enumerate(regs):
                o_ref[q * per + gl, cl, :, hf * LANES:(hf + 1) * LANES] = r.astype(BF16)


def _from_chunks_kernel(y_ref, o_ref, s_ref, *, n_h):
    nb, ntok, width = o_ref.shape
    t_chunk = y_ref.shape[3] // n_h
    pitch = s_ref.shape[1] // nb
    for cl, hf, qs, per, tok in _chunk_tiles(ntok, width, n_h, t_chunk):
        groups = [[y_ref[q * per + gl, cl, :, hf * LANES:(hf + 1) * LANES].astype(F32) for gl in range(per)]
                  for q in qs]
        for q, regs in zip(qs, _block_transpose(groups, n_h)):
            for j, r in enumerate(regs):
                s_ref[q, pl.ds(tok + j, nb, stride=pitch), :] = r
    for q in range(width // LANES):
        for b in range(nb):
            o_ref[b, :, q * LANES:(q + 1) * LANES] = s_ref[q, b * pitch:b * pitch + ntok, :].astype(BF16)


def _chunk_specs(bsz, seq, width, n_g):
    n_h = width // n_g
    nb = min(RELAYOUT_BATCH, bsz)
    ntok = RELAYOUT_CHUNKS * SSM_CHUNK
    assert bsz % nb == 0 and seq % ntok == 0 and LANES % n_h == 0 and width % LANES == 0
    nat = pl.BlockSpec((nb, ntok, width), lambda bi, ti: (bi, ti, 0))
    chunked = pl.BlockSpec((n_g, RELAYOUT_CHUNKS, nb, SSM_CHUNK * n_h), lambda bi, ti: (0, ti, bi, 0))
    pitch = ntok + SUBLANES if (ntok // SUBLANES) % 2 == 0 else ntok
    scratch = pltpu.VMEM((width // LANES, nb * pitch, LANES), F32)
    return dict(grid=(bsz // nb, seq // ntok), scratch_shapes=[scratch],
                compiler_params=_params("parallel", "parallel")), nat, chunked, n_h


def _to_chunks(u, n_g):
    bsz, seq, width = u.shape
    common, nat, chunked, n_h = _chunk_specs(bsz, seq, width, n_g)
    out = pl.pallas_call(
        functools.partial(_to_chunks_kernel, n_h=n_h), in_specs=[nat], out_specs=chunked,
        out_shape=jax.ShapeDtypeStruct((n_g, seq // SSM_CHUNK, bsz, SSM_CHUNK * n_h), BF16),
        name="to_chunks", **common)(u)
    return out.reshape(n_g, (seq // SSM_CHUNK) * bsz, SSM_CHUNK * n_h)


def _from_chunks(y, bsz):
    n_g, rows, th = y.shape
    n_h = th // SSM_CHUNK
    seq = rows // bsz * SSM_CHUNK
    common, nat, chunked, n_h = _chunk_specs(bsz, seq, n_g * n_h, n_g)
    return pl.pallas_call(
        functools.partial(_from_chunks_kernel, n_h=n_h), in_specs=[chunked], out_specs=nat,
        out_shape=jax.ShapeDtypeStruct((bsz, seq, n_g * n_h), BF16),
        name="from_chunks", **common)(y.reshape(n_g, seq // SSM_CHUNK, bsz, th))


def _ssm_weights(lam_re, lam_im, log_dt, b_re, b_im, c_re, c_im, d_skip, t_chunk):
    n_dir, n_g, n_p = lam_re.shape
    n_h = b_re.shape[-1]
    dt = jnp.exp(log_dt)[..., None]
    xr, xi = lam_re * dt, lam_im * dt
    mag = jnp.exp(xr)
    a_re, a_im = mag * jnp.cos(xi), mag * jnp.sin(xi)
    n_re = jnp.expm1(xr) * jnp.cos(xi) - 2.0 * jnp.sin(0.5 * xi) ** 2
    n_im = a_im
    den = lam_re * lam_re + lam_im * lam_im
    q_re = (n_re * lam_re + n_im * lam_im) / den
    q_im = (n_im * lam_re - n_re * lam_im) / den
    bb_re = q_re[..., None] * b_re - q_im[..., None] * b_im
    bb_im = q_re[..., None] * b_im + q_im[..., None] * b_re

    pw_re, pw_im = [jnp.ones_like(a_re)], [jnp.zeros_like(a_im)]
    for _ in range(t_chunk):
        pr, pi = pw_re[-1], pw_im[-1]
        pw_re.append(pr * a_re - pi * a_im)
        pw_im.append(pr * a_im + pi * a_re)
    pw_re, pw_im = jnp.stack(pw_re), jnp.stack(pw_im)

    t = jnp.arange(t_chunk)
    e_in = jnp.stack([t_chunk - 1 - t, t], axis=1)
    e_out = jnp.stack([t + 1, t_chunk - t], axis=1)
    dsel = jnp.arange(n_dir)[None, :]

    pin_re, pin_im = pw_re[e_in, dsel], pw_im[e_in, dsel]
    win_re = pin_re[..., None] * bb_re[None] - pin_im[..., None] * bb_im[None]
    win_im = pin_re[..., None] * bb_im[None] + pin_im[..., None] * bb_re[None]

    def to_in(w, k):
        return jnp.transpose(w[:, k], (1, 0, 3, 2)).reshape(n_g, t_chunk * n_h, n_p)

    w_in = jnp.concatenate([to_in(win_re, 0), to_in(win_re, 1), to_in(win_im, 0), to_in(win_im, 1)], axis=-1)

    pout_re, pout_im = pw_re[e_out, dsel], pw_im[e_out, dsel]
    cr, ci = c_re[None], c_im[None]
    e_re = cr * pout_re[:, :, :, None, :] - ci * pout_im[:, :, :, None, :]
    e_im = cr * pout_im[:, :, :, None, :] + ci * pout_re[:, :, :, None, :]

    def to_out(w, k):
        return jnp.transpose(w[:, k], (1, 3, 0, 2)).reshape(n_g, n_p, t_chunk * n_h)

    w_out = jnp.concatenate([to_out(e_re, 0), to_out(e_re, 1), to_out(-e_im, 0), to_out(-e_im, 1)], axis=1)

    g_re = pw_re[:t_chunk, ..., None] * bb_re[None] - pw_im[:t_chunk, ..., None] * bb_im[None]
    g_im = pw_re[:t_chunk, ..., None] * bb_im[None] + pw_im[:t_chunk, ..., None] * bb_re[None]
    kk = (jnp.einsum('dgip,tdgpj->tdgij', c_re, g_re, precision=HIGHEST)
          - jnp.einsum('dgip,tdgpj->tdgij', c_im, g_im, precision=HIGHEST))
    tau = t[None, :] - t[:, None]
    k_f = kk[jnp.clip(tau, 0, None), 0]
    k_b = kk[jnp.clip(-tau, 0, None), 1]
    k_0 = kk[0, 0] + kk[0, 1] + jnp.eye(n_h, dtype=F32)[None] * d_skip.reshape(n_g, n_h, 1)
    tau5 = tau[:, :, None, None, None]
    m = jnp.where(tau5 > 0, k_f, jnp.where(tau5 < 0, k_b, k_0[None, None]))
    m = jnp.transpose(m, (2, 0, 4, 1, 3)).reshape(n_g, t_chunk * n_h, t_chunk * n_h)

    at_re = jnp.concatenate([pw_re[t_chunk, 0], pw_re[t_chunk, 1]], axis=-1)[:, None, :]
    at_im = jnp.concatenate([pw_im[t_chunk, 0], pw_im[t_chunk, 1]], axis=-1)[:, None, :]
    return w_in.astype(BF16), m.astype(BF16), w_out.astype(BF16), at_re, at_im


def _ssm_kernel(uc_ref, ul_ref, win_ref, m_ref, wout_ref, are_ref, aim_ref, y_ref, s_ref, xin_ref,
                *, nb, n_ctx, n_lat):
    n_all = n_ctx + n_lat
    lat = n_ctx * nb
    half = s_ref.shape[1] // 2
    quarter = half // 2
    s_ref[0:lat, :] = jnp.dot(uc_ref[0], win_ref[0], preferred_element_type=F32)
    s_ref[lat:, :] = jnp.dot(ul_ref[0], win_ref[0], preferred_element_type=F32)
    a_re = are_ref[0]
    a_im = aim_ref[0]
    is_f = lax.broadcasted_iota(jnp.int32, (nb, half), 1) < quarter

    def step(k, carry):
        x_re, x_im = carry
        cb = jnp.where(k < n_ctx, n_ctx - 1 - k, n_all + n_ctx - 1 - k)
        rf = pl.multiple_of(k * nb, nb)
        rb = pl.multiple_of(cb * nb, nb)
        xin_ref[pl.ds(rf, nb), 0:quarter] = x_re[:, 0:quarter]
        xin_ref[pl.ds(rb, nb), quarter:half] = x_re[:, quarter:half]
        xin_ref[pl.ds(rf, nb), half:half + quarter] = x_im[:, 0:quarter]
        xin_ref[pl.ds(rb, nb), half + quarter:2 * half] = x_im[:, quarter:half]
        s_f = s_ref[pl.ds(rf, nb), :]
        s_b = s_ref[pl.ds(rb, nb), :]
        s_re = jnp.where(is_f, s_f[:, :half], s_b[:, :half])
        s_im = jnp.where(is_f, s_f[:, half:], s_b[:, half:])
        return (a_re * x_re - a_im * x_im + s_re, a_re * x_im + a_im * x_re + s_im)

    zero = jnp.zeros((nb, half), F32)
    lax.fori_loop(0, n_all, step, (zero, zero))

    y = jnp.dot(ul_ref[0], m_ref[0], preferred_element_type=F32)
    y = y + jnp.dot(xin_ref[lat:, :].astype(BF16), wout_ref[0], preferred_element_type=F32)
    y_ref[0] = y.astype(BF16)


def _ssm(u_ctx, u_lat, w_in, m, w_out, at_re, at_im, nb):
    n_g, rows_lat, th = u_lat.shape
    rows = u_ctx.shape[1] + rows_lat
    ns = w_in.shape[2]
    per_g = lambda a: pl.BlockSpec((1,) + a.shape[1:], lambda g: (g, 0, 0))
    ins = (u_ctx, u_lat, w_in, m, w_out, at_re, at_im)
    return pl.pallas_call(
        functools.partial(_ssm_kernel, nb=nb, n_ctx=u_ctx.shape[1] // nb, n_lat=rows_lat // nb),
        grid=(n_g,),
        in_specs=[per_g(a) for a in ins],
        out_specs=pl.BlockSpec((1, rows_lat, th), lambda g: (g, 0, 0)),
        out_shape=jax.ShapeDtypeStruct((n_g, rows_lat, th), BF16),
        scratch_shapes=[pltpu.VMEM((rows, ns), F32), pltpu.VMEM((rows, ns), F32)],
        compiler_params=_params("parallel"),
        name="ssm",
    )(*ins)


def _merge_kernel(ya_ref, bg_ref, cx_ref, cxp_ref, cxn_ref, ga_ref, gb_ref, x_ref,
                  g1_ref, sh2_ref, sc2_ref, gluw_ref, glub_ref, scw_ref, pa_ref, pb_ref, wo_ref, n2_ref,
                  x1_ref, h2_ref):
    i = pl.program_id(1)
    tm = cx_ref.shape[1]
    z = jax.nn.gelu(ya_ref[0].astype(F32), approximate=True)
    gate = jax.nn.sigmoid(jnp.dot(z.astype(BF16), gluw_ref[...], preferred_element_type=F32) + glub_ref[...])
    y_a = (z * gate).astype(BF16)

    cx = cx_ref[0].astype(F32)
    prev = cxp_ref[0].astype(F32)[BF16_SUBLANES - 1:BF16_SUBLANES]
    nxt = cxn_ref[0].astype(F32)[0:1]
    prev = jnp.where(i > 0, prev, 0.0)
    nxt = jnp.where(i < pl.num_programs(1) - 1, nxt, 0.0)
    row = lax.broadcasted_iota(jnp.int32, (tm, 1), 0)
    c_m1 = jnp.where(row == 0, prev, pltpu.roll(cx, 1, 0))
    c_p1 = jnp.where(row == tm - 1, nxt, pltpu.roll(cx, tm - 1, 0))
    w = scw_ref[...]
    y_b = (bg_ref[0].astype(F32) * (w[0:1] * c_m1 + w[1:2] * cx + w[2:3] * c_p1)).astype(BF16)

    merged = (ga_ref[0].astype(F32) * jnp.dot(y_a, pa_ref[...], preferred_element_type=F32)
              + gb_ref[0].astype(F32) * jnp.dot(y_b, pb_ref[...], preferred_element_type=F32))
    x1 = x_ref[0] + g1_ref[0] * jnp.dot(merged.astype(BF16), wo_ref[...], preferred_element_type=F32)
    x1_ref[0] = x1
    h2_ref[0] = ((x1 * _rms(x1)) * n2_ref[...] * (1.0 + sc2_ref[0]) + sh2_ref[0]).astype(BF16)


def _merge(ya, bg, cx, ga, gb, x, mods3, glu_w, glu_b, sconv_w, proj_a, proj_b, w_out, n2, tm):
    bsz, seq, d = x.shape
    sw, cw = ya.shape[2], cx.shape[2]
    hb = BF16_SUBLANES
    last_hb = seq // hb - 1
    tok = lambda n: pl.BlockSpec((1, tm, n), lambda b, i: (b, i, 0))
    mod = lambda k: pl.BlockSpec((1, 1, d), lambda b, i: (b, 0, k))
    full = lambda a: pl.BlockSpec(a.shape, lambda b, i: (0,) * a.ndim)
    prev_spec = pl.BlockSpec((1, hb, cw), lambda b, i: (b, jnp.maximum(i * (tm // hb) - 1, 0), 0))
    next_spec = pl.BlockSpec((1, hb, cw), lambda b, i: (b, jnp.minimum((i + 1) * (tm // hb), last_hb), 0))
    consts = (glu_w, glu_b, sconv_w, proj_a, proj_b, w_out, n2)
    return pl.pallas_call(
        _merge_kernel,
        grid=(bsz, seq // tm),
        in_specs=[tok(sw), tok(cw), tok(cw), prev_spec, next_spec, tok(d), tok(d), tok(d),
                  mod(2), mod(3), mod(4)] + [full(a) for a in consts],
        out_specs=[tok(d), tok(d)],
        out_shape=[jax.ShapeDtypeStruct((bsz, seq, d), F32), jax.ShapeDtypeStruct((bsz, seq, d), BF16)],
        compiler_params=_params("parallel", "parallel"),
        name="merge",
    )(ya, bg, cx, cx, cx, ga, gb, x, mods3, mods3, mods3, *consts)


FFN_ROW_BLOCK = 256


def _ffn_kernel(h_ref, x1_ref, g2_ref, fg_ref, wa_ref, wv_ref, cwa_ref, cwv_ref, wd_ref, o_ref,
                upa0_ref, upv0_ref, upa1_ref, upv1_ref, act0_ref, act1_ref, *, gw, nj, n_steps):
    s = pl.program_id(0)
    seq, ck = act0_ref.shape
    d = o_ref.shape[2]
    rb = min(FFN_ROW_BLOCK, seq)
    nt = min(MXU_TILE, d)
    j2 = jnp.clip(s - 2, 0, n_steps - 1) % nj

    @pl.when(s == 0)
    def _():
        for ref in (upa0_ref, upv0_ref, upa1_ref, upv1_ref, act0_ref, act1_ref):
            ref[...] = jnp.zeros(ref.shape, BF16)

    @pl.when(j2 == 0)
    def _():
        o_ref[0] = jnp.zeros(o_ref.shape[1:], F32)

    col = lax.broadcasted_iota(jnp.int32, (gw, ck), 0)
    first_col = col == 0
    last_col = col == gw - 1

    def stages(up_w, up_r, act_w, act_r):
        def up_piece(lo, w_ref, dst):
            def run():
                u = jnp.dot(h_ref[0, lo:lo + rb, :], w_ref[0], preferred_element_type=F32)
                for g0 in range(0, rb, gw):
                    ug = u[g0:g0 + gw, :]
                    r = gw + lo + g0
                    dst[0, r:r + gw, :] = jnp.where(first_col, 0.0, pltpu.roll(ug, 1, 0)).astype(BF16)
                    dst[1, r:r + gw, :] = ug.astype(BF16)
                    dst[2, r:r + gw, :] = jnp.where(last_col, 0.0, pltpu.roll(ug, gw - 1, 0)).astype(BF16)
            return run

        def down_piece(lo, n0):
            def run():
                o_ref[0, lo:lo + rb, n0:n0 + nt] += jnp.dot(act_r[lo:lo + rb, :], wd_ref[:, n0:n0 + nt],
                                                            preferred_element_type=F32)
            return run

        cw = [cwa_ref[0].astype(BF16), cwv_ref[0].astype(BF16)]

        def conv_piece(base, l0):
            def run():
                acc = []
                for src, w in zip(up_r, cw):
                    t = None
                    for dy in range(3):
                        for dx in range(3):
                            k = dy * 3 + dx
                            term = w[k:k + 1, l0:l0 + LANES] * src[dx, base + dy * gw:base + (dy + 1) * gw,
                                                                   l0:l0 + LANES]
                            t = term if t is None else t + term
                    acc.append(t)
                a, v = acc
                act_w[base:base + gw, l0:l0 + LANES] = a * jax.nn.sigmoid(a) * v
            return run

        mxu = []
        for lo in range(0, seq, rb):
            mxu.append((rb * d * ck, up_piece(lo, wa_ref, up_w[0])))
            mxu.append((rb * d * ck, up_piece(lo, wv_ref, up_w[1])))
            mxu += [(rb * ck * nt, down_piece(lo, n0)) for n0 in range(0, d, nt)]
        vpu = [conv_piece(base, l0) for base in range(0, seq, gw) for l0 in range(0, ck, LANES)]
        total = sum(c for c, _ in mxu)
        done, k = 0, 0
        for c, piece in mxu:
            piece()
            done += c
            while k < len(vpu) and (k + 1) * total <= done * len(vpu):
                vpu[k]()
                k += 1
        for piece in vpu[k:]:
            piece()

    @pl.when(s % 2 == 0)
    def _():
        stages((upa0_ref, upv0_ref), (upa1_ref, upv1_ref), act1_ref, act0_ref)

    @pl.when(s % 2 == 1)
    def _():
        stages((upa1_ref, upv1_ref), (upa0_ref, upv0_ref), act0_ref, act1_ref)

    @pl.when(jnp.logical_and(s >= 2, j2 == nj - 1))
    def _():
        x2 = x1_ref[0] + g2_ref[0] * o_ref[0]
        o_ref[0] = (x2 * _rms(x2)) * fg_ref[...]


def _ffn(h2, x1, mods3, final_g, w_up, conv_w, w_down, ck):
    bsz, seq, d = x1.shape
    hid = w_down.shape[0]
    nj = hid // ck
    n_steps = bsz * nj
    taps = conv_w.shape[0]
    at = lambda s, lag: jnp.clip(s - lag, 0, n_steps - 1)
    w_up = jnp.transpose(w_up.reshape(d, 2 * nj, ck), (1, 0, 2))
    conv_w = jnp.transpose(conv_w.reshape(taps, 2 * nj, ck), (1, 0, 2))
    return pl.pallas_call(
        functools.partial(_ffn_kernel, gw=GRID_W, nj=nj, n_steps=n_steps),
        grid=(n_steps + 2,),
        in_specs=[pl.BlockSpec((1, seq, d), lambda s: (at(s, 0) // nj, 0, 0)),
                  pl.BlockSpec((1, seq, d), lambda s: (at(s, 2) // nj, 0, 0), pipeline_mode=pl.Buffered(1)),
                  pl.BlockSpec((1, 1, d), lambda s: (at(s, 2) // nj, 0, 5)),
                  pl.BlockSpec((1, d), lambda s: (0, 0)),
                  pl.BlockSpec((1, d, ck), lambda s: (at(s, 0) % nj, 0, 0)),
                  pl.BlockSpec((1, d, ck), lambda s: (nj + at(s, 0) % nj, 0, 0)),
                  pl.BlockSpec((1, taps, ck), lambda s: (at(s, 1) % nj, 0, 0)),
                  pl.BlockSpec((1, taps, ck), lambda s: (nj + at(s, 1) % nj, 0, 0)),
                  pl.BlockSpec((ck, d), lambda s: (at(s, 2) % nj, 0))],
        out_specs=pl.BlockSpec((1, seq, d), lambda s: (at(s, 2) // nj, 0, 0)),
        out_shape=jax.ShapeDtypeStruct((bsz, seq, d), F32),
        scratch_shapes=[pltpu.VMEM((3, seq + 2 * GRID_W, ck), BF16)] * 4 + [pltpu.VMEM((seq, ck), BF16)] * 2,
        compiler_params=_params("arbitrary"),
        name="ffn",
    )(h2, x1, mods3, final_g, w_up, w_up, conv_w, conv_w, w_down)


def _layer(x, ctx, mods3, ctx_row, norm1_g, norm2_g, w_in, ssm, glu_w, glu_b, sconv_w,
           proj_a, proj_b, w_out, ffn_w_up, ffn_conv_w, ffn_w_down, out_g):
    bsz, seq, d = x.shape
    ctx_len = ctx.shape[1]
    n_g, th = ssm[1].shape[0], ssm[1].shape[1]
    sw = glu_w.shape[0]
    cw = sconv_w.shape[1]
    n_h = sw // n_g
    t_chunk = th // n_h
    n_ctx, n_lat = ctx_len // t_chunk, seq // t_chunk
    tm = min(512, seq)
    w_in_b = w_in.astype(BF16)

    u, bg, cx, ga, gb = _inproj(x, mods3, norm1_g, w_in_b, sw, cw, tm)
    u_ctx = _ctxproj(ctx, mods3, ctx_row, norm1_g, w_in_b[:, :sw], min(tm, ctx_len))
    y = _ssm(_to_chunks(u_ctx, n_g), _to_chunks(u, n_g), *ssm, bsz)
    ya = _from_chunks(y, bsz)

    x1, h2 = _merge(ya, bg, cx, ga, gb, x, mods3, glu_w.astype(BF16), glu_b.reshape(1, sw), sconv_w,
                    proj_a.astype(BF16), proj_b.astype(BF16), w_out.astype(BF16), norm2_g, tm)
    k2 = ffn_conv_w.shape[0] * ffn_conv_w.shape[1]
    return _ffn(h2, x1, mods3, out_g, ffn_w_up.astype(BF16), ffn_conv_w.reshape(k2, -1),
                ffn_w_down.astype(BF16), 256)


def kernel(x, c, ctx, c_ctx, mod_w, mod_b, norm1_g, norm2_g, w_in, ssm_lambda_re, ssm_lambda_im,
           ssm_log_dt, ssm_b_re, ssm_b_im, ssm_c_re, ssm_c_im, ssm_d, ssm_glu_w, ssm_glu_b,
           sconv_w, proj_a, proj_b, w_out, ffn_w_up, ffn_conv_w, ffn_w_down, final_g):
    depth = mod_w.shape[0]
    assert depth == 1, "context-stream update between layers is not implemented"
    bsz, seq, d = x.shape
    assert seq % GRID_W == 0 and seq % SSM_CHUNK == 0 and ctx.shape[1] % SSM_CHUNK == 0
    i = 0
    rows = -(-(bsz + 1) // 8) * 8
    cond = jnp.zeros((rows, d), F32).at[:bsz].set(c).at[bsz].set(c_ctx)
    mods3 = _adaln(cond, mod_w[i], mod_b[i]).reshape(rows, 1, N_MOD * d)
    ssm = _ssm_weights(ssm_lambda_re[i], ssm_lambda_im[i], ssm_log_dt[i], ssm_b_re[i], ssm_b_im[i],
                       ssm_c_re[i], ssm_c_im[i], ssm_d[i], SSM_CHUNK)
    return _layer(x, ctx, mods3, bsz, norm1_g[i].reshape(1, d), norm2_g[i].reshape(1, d), w_in[i], ssm,
                  ssm_glu_w[i], ssm_glu_b[i], sconv_w[i], proj_a[i], proj_b[i], w_out[i],
                  ffn_w_up[i], ffn_conv_w[i], ffn_w_down[i], final_g.reshape(1, d))
```

```python
import functools

import jax
import jax.numpy as jnp
from jax import lax
from jax.experimental import pallas as pl
from jax.experimental.pallas import tpu as pltpu

F32 = jnp.float32
BF16 = jnp.bfloat16

EPS = 1e-6
GRID_W = 64
N_MOD = 6
SSM_CHUNK = 16
SUBLANES = 8
LANES = 128
BF16_SUBLANES = 16
MXU_TILE = 256
VMEM_LIMIT_BYTES = 56 * 1024 * 1024
HIGHEST = lax.Precision.HIGHEST


def _params(*semantics):
    return pltpu.CompilerParams(dimension_semantics=semantics, vmem_limit_bytes=VMEM_LIMIT_BYTES)


def _rms(x):
    return lax.rsqrt(jnp.mean(x * x, axis=-1, keepdims=True) + EPS)


def _adaln_kernel(c_ref, w_ref, b_ref, o_ref):
    s = c_ref[...]
    s = s * jax.nn.sigmoid(s)
    o_ref[...] = jnp.dot(s, w_ref[...], preferred_element_type=F32, precision=HIGHEST) + b_ref[...]


def _adaln(cond, w, b):
    r, d = cond.shape
    n = w.shape[1]
    tn = d
    return pl.pallas_call(
        _adaln_kernel,
        grid=(n // tn,),
        in_specs=[pl.BlockSpec((r, d), lambda j: (0, 0)),
                  pl.BlockSpec((d, tn), lambda j: (0, j)),
                  pl.BlockSpec((1, tn), lambda j: (0, j))],
        out_specs=pl.BlockSpec((r, tn), lambda j: (0, j)),
        out_shape=jax.ShapeDtypeStruct((r, n), F32),
        compiler_params=_params("parallel"),
        name="adaln",
    )(cond, w, b.reshape(1, n))


def _norm_mod(x_ref, sh_ref, sc_ref, g_ref):
    x = x_ref[0]
    return ((x * _rms(x)) * g_ref[...] * (1.0 + sc_ref[0]) + sh_ref[0]).astype(BF16)


def _inproj_kernel(x_ref, sh_ref, sc_ref, g_ref, w_ref, u_ref, bg_ref, cx_ref, ga_ref, gb_ref,
                   *, sw, cw, d):
    hb = _norm_mod(x_ref, sh_ref, sc_ref, g_ref)

    def proj(lo, n):
        return jnp.dot(hb, w_ref[:, lo:lo + n], preferred_element_type=F32)

    u_ref[0] = proj(0, sw).astype(BF16)
    bg_ref[0] = proj(sw, cw).astype(BF16)
    cx_ref[0] = (proj(sw + cw, cw) * proj(sw + 2 * cw, cw)).astype(BF16)
    o2 = sw + 3 * cw
    ga_ref[0] = jax.nn.sigmoid(proj(o2, d)).astype(BF16)
    gb_ref[0] = jax.nn.sigmoid(proj(o2 + d, d)).astype(BF16)


def _inproj(x, mods3, g, w, sw, cw, tm):
    bsz, seq, d = x.shape
    cols = w.shape[1]
    tok = lambda n: pl.BlockSpec((1, tm, n), lambda b, i: (b, i, 0))
    mod = lambda k: pl.BlockSpec((1, 1, d), lambda b, i: (b, 0, k))
    out = lambda n: jax.ShapeDtypeStruct((bsz, seq, n), BF16)
    return pl.pallas_call(
        functools.partial(_inproj_kernel, sw=sw, cw=cw, d=d),
        grid=(bsz, seq // tm),
        in_specs=[tok(d), mod(0), mod(1),
                  pl.BlockSpec((1, d), lambda b, i: (0, 0)),
                  pl.BlockSpec((d, cols), lambda b, i: (0, 0))],
        out_specs=[tok(sw), tok(cw), tok(cw), tok(d), tok(d)],
        out_shape=[out(sw), out(cw), out(cw), out(d), out(d)],
        compiler_params=_params("parallel", "parallel"),
        name="inproj",
    )(x, mods3, mods3, g, w)


def _ctxproj_kernel(x_ref, sh_ref, sc_ref, g_ref, w_ref, u_ref):
    hb = _norm_mod(x_ref, sh_ref, sc_ref, g_ref)
    u_ref[0] = jnp.dot(hb, w_ref[...], preferred_element_type=F32).astype(BF16)


def _ctxproj(ctx, mods3, row, g, w, tm):
    bsz, seq, d = ctx.shape
    sw = w.shape[1]
    mod = lambda k: pl.BlockSpec((1, 1, d), lambda b, i: (row, 0, k))
    return pl.pallas_call(
        _ctxproj_kernel,
        grid=(bsz, seq // tm),
        in_specs=[pl.BlockSpec((1, tm, d), lambda b, i: (b, i, 0)), mod(0), mod(1),
                  pl.BlockSpec((1, d), lambda b, i: (0, 0)),
                  pl.BlockSpec((d, sw), lambda b, i: (0, 0))],
        out_specs=pl.BlockSpec((1, tm, sw), lambda b, i: (b, i, 0)),
        out_shape=jax.ShapeDtypeStruct((bsz, seq, sw), BF16),
        compiler_params=_params("parallel", "parallel"),
        name="ctxproj",
    )(ctx, mods3, mods3, g, w)


RELAYOUT_BATCH = BF16_SUBLANES
RELAYOUT_CHUNKS = 2
RELAYOUT_TILES_IN_FLIGHT = 4


def _block_transpose(groups, width):
    n = len(groups[0])
    shape = groups[0][0].shape
    lanes = shape[-1]
    block = lax.broadcasted_iota(jnp.int32, shape, len(shape) - 1) // width
    s = n // 2
    while s:
        keep = (block & s) == 0
        nxt = []
        for regs in groups:
            new = list(regs)
            for i in range(n):
                if i & s == 0:
                    a, b = regs[i], regs[i | s]
                    new[i] = jnp.where(keep, a, pltpu.roll(b, s * width, 1))
                    new[i | s] = jnp.where(keep, pltpu.roll(a, lanes - s * width, 1), b)
            nxt.append(new)
        groups = nxt
        s //= 2
    return groups


def _chunk_tiles(ntok, width, n_h, t_chunk):
    per = LANES // n_h
    n_q = width // LANES
    step = min(RELAYOUT_TILES_IN_FLIGHT, n_q)
    for cl in range(ntok // t_chunk):
        for hf in range(t_chunk // per):
            for q0 in range(0, n_q, step):
                yield cl, hf, range(q0, q0 + step), per, cl * t_chunk + hf * per


def _to_chunks_kernel(u_ref, o_ref, s_ref, *, n_h):
    nb, ntok, width = u_ref.shape
    t_chunk = o_ref.shape[3] // n_h
    pitch = s_ref.shape[1] // nb
    for q in range(width // LANES):
        for b in range(nb):
            s_ref[q, b * pitch:b * pitch + ntok, :] = u_ref[b, :, q * LANES:(q + 1) * LANES].astype(F32)
    for cl, hf, qs, per, tok in _chunk_tiles(ntok, width, n_h, t_chunk):
        groups = [[s_ref[q, pl.ds(tok + j, nb, stride=pitch), :] for j in range(per)] for q in qs]
        for q, regs in zip(qs, _block_transpose(groups, n_h)):
            for gl, r in enumerate(regs):
                o_ref[q * per + gl, cl, :, hf * LANES:(hf + 1) * LANES] = r.astype(BF16)


def _from_chunks_kernel(y_ref, o_ref, s_ref, *, n_h):
    nb, ntok, width = o_ref.shape
    t_chunk = y_ref.shape[3] // n_h
    pitch = s_ref.shape[1] // nb
    for cl, hf, qs, per, tok in _chunk_tiles(ntok, width, n_h, t_chunk):
        groups = [[y_ref[q * per + gl, cl, :, hf * LANES:(hf + 1) * LANES].astype(F32) for gl in range(per)]
                  for q in qs]
        for q, regs in zip(qs, _block_transpose(groups, n_h)):
            for j, r in enumerate(regs):
                s_ref[q, pl.ds(tok + j, nb, stride=pitch), :] = r
    for q in range(width // LANES):
        for b in range(nb):
            o_ref[b, :, q * LANES:(q + 1) * LANES] = s_ref[q, b * pitch:b * pitch + ntok, :].astype(BF16)


def _chunk_specs(bsz, seq, width, n_g):
    n_h = width // n_g
    nb = min(RELAYOUT_BATCH, bsz)
    ntok = RELAYOUT_CHUNKS * SSM_CHUNK
    assert bsz % nb == 0 and seq % ntok == 0 and LANES % n_h == 0 and width % LANES == 0
    nat = pl.BlockSpec((nb, ntok, width), lambda bi, ti: (bi, ti, 0))
    chunked = pl.BlockSpec((n_g, RELAYOUT_CHUNKS, nb, SSM_CHUNK * n_h), lambda bi, ti: (0, ti, bi, 0))
    pitch = ntok + SUBLANES if (ntok // SUBLANES) % 2 == 0 else ntok
    scratch = pltpu.VMEM((width // LANES, nb * pitch, LANES), F32)
    return dict(grid=(bsz // nb, seq // ntok), scratch_shapes=[scratch],
                compiler_params=_params("parallel", "parallel")), nat, chunked, n_h


def _to_chunks(u, n_g):
    bsz, seq, width = u.shape
    common, nat, chunked, n_h = _chunk_specs(bsz, seq, width, n_g)
    return pl.pallas_call(
        functools.partial(_to_chunks_kernel, n_h=n_h), in_specs=[nat], out_specs=chunked,
        out_shape=jax.ShapeDtypeStruct((n_g, seq // SSM_CHUNK, bsz, SSM_CHUNK * n_h), BF16),
        name="to_chunks", **common)(u)


def _from_chunks(y):
    n_g, n_chunks, bsz, th = y.shape
    n_h = th // SSM_CHUNK
    seq = n_chunks * SSM_CHUNK
    common, nat, chunked, n_h = _chunk_specs(bsz, seq, n_g * n_h, n_g)
    return pl.pallas_call(
        functools.partial(_from_chunks_kernel, n_h=n_h), in_specs=[chunked], out_specs=nat,
        out_shape=jax.ShapeDtypeStruct((bsz, seq, n_g * n_h), BF16),
        name="from_chunks", **common)(y)


def _lag_kernel(e_ref, b_ref, o_ref):
    for dr in range(e_ref.shape[0]):
        o_ref[dr, 0] = jnp.dot(e_ref[dr, 0], b_ref[dr, 0], preferred_element_type=F32, precision=HIGHEST)


def _lag_kernels(lhs, rhs):
    n_dir, n_g, rows, k = lhs.shape
    n_h = rhs.shape[3]
    spec = lambda r, c: pl.BlockSpec((n_dir, 1, r, c), lambda g: (0, g, 0, 0))
    return pl.pallas_call(
        _lag_kernel,
        grid=(n_g,),
        in_specs=[spec(rows, k), spec(k, n_h)],
        out_specs=spec(rows, n_h),
        out_shape=jax.ShapeDtypeStruct((n_dir, n_g, rows, n_h), F32),
        compiler_params=_params("parallel"),
        name="lag_kernels",
    )(lhs, rhs)


def _ssm_weights(lam_re, lam_im, log_dt, b_re, b_im, c_re, c_im, d_skip, t_chunk):
    n_dir, n_g, n_p = lam_re.shape
    n_h = b_re.shape[-1]
    dt = jnp.exp(log_dt)[..., None]
    xr, xi = lam_re * dt, lam_im * dt
    mag = jnp.exp(xr)
    a_re, a_im = mag * jnp.cos(xi), mag * jnp.sin(xi)
    n_re = jnp.expm1(xr) * jnp.cos(xi) - 2.0 * jnp.sin(0.5 * xi) ** 2
    n_im = a_im
    den = lam_re * lam_re + lam_im * lam_im
    q_re = (n_re * lam_re + n_im * lam_im) / den
    q_im = (n_im * lam_re - n_re * lam_im) / den
    bb_re = q_re[..., None] * b_re - q_im[..., None] * b_im
    bb_im = q_re[..., None] * b_im + q_im[..., None] * b_re

    pw_re, pw_im = [jnp.ones_like(a_re)], [jnp.zeros_like(a_im)]
    for _ in range(t_chunk):
        pr, pi = pw_re[-1], pw_im[-1]
        pw_re.append(pr * a_re - pi * a_im)
        pw_im.append(pr * a_im + pi * a_re)
    pw_re, pw_im = jnp.stack(pw_re), jnp.stack(pw_im)

    def by_token(pw, lo, flip_dir):
        p = pw[lo:lo + t_chunk]
        return jnp.stack([p[::-1, d] if d == flip_dir else p[:, d] for d in range(n_dir)], axis=1)

    pin_re, pin_im = by_token(pw_re, 0, 0), by_token(pw_im, 0, 0)
    win_re = pin_re[..., None] * bb_re[None] - pin_im[..., None] * bb_im[None]
    win_im = pin_re[..., None] * bb_im[None] + pin_im[..., None] * bb_re[None]

    def to_in(w, k):
        return jnp.transpose(w[:, k], (1, 0, 3, 2)).reshape(n_g, t_chunk * n_h, n_p)

    w_in = jnp.concatenate([to_in(win_re, 0), to_in(win_re, 1), to_in(win_im, 0), to_in(win_im, 1)], axis=-1)

    pout_re, pout_im = by_token(pw_re, 1, 1), by_token(pw_im, 1, 1)
    cr, ci = c_re[None], c_im[None]
    e_re = cr * pout_re[:, :, :, None, :] - ci * pout_im[:, :, :, None, :]
    e_im = cr * pout_im[:, :, :, None, :] + ci * pout_re[:, :, :, None, :]

    def to_out(w, k):
        return jnp.transpose(w[:, k], (1, 3, 0, 2)).reshape(n_g, n_p, t_chunk * n_h)

    w_out = jnp.concatenate([to_out(e_re, 0), to_out(e_re, 1), to_out(-e_im, 0), to_out(-e_im, 1)], axis=1)

    ca_re = cr * pw_re[:t_chunk, :, :, None, :] - ci * pw_im[:t_chunk, :, :, None, :]
    ca_im = cr * pw_im[:t_chunk, :, :, None, :] + ci * pw_re[:t_chunk, :, :, None, :]
    lhs = jnp.concatenate([ca_re, -ca_im], axis=-1)
    lhs = jnp.transpose(lhs, (1, 2, 0, 3, 4)).reshape(n_dir, n_g, t_chunk * n_h, 2 * n_p)
    kk = _lag_kernels(lhs, jnp.concatenate([bb_re, bb_im], axis=2))
    kk_t = jnp.transpose(kk.reshape(n_dir, n_g, t_chunk, n_h, n_h), (0, 1, 4, 2, 3))
    k_0 = kk_t[0, :, :, 0] + kk_t[1, :, :, 0] + jnp.eye(n_h, dtype=F32)[None] * d_skip.reshape(n_g, n_h, 1)
    strip = jnp.concatenate([kk_t[1, :, :, :0:-1].reshape(n_g, n_h, -1), k_0,
                             kk_t[0, :, :, 1:].reshape(n_g, n_h, -1)], axis=-1)
    th = t_chunk * n_h
    m = jnp.stack([strip[:, :, (t_chunk - 1 - t) * n_h:(t_chunk - 1 - t) * n_h + th] for t in range(t_chunk)],
                  axis=1).reshape(n_g, th, th)

    at_re = jnp.concatenate([pw_re[t_chunk, 0], pw_re[t_chunk, 1]], axis=-1)[:, None, :]
    at_im = jnp.concatenate([pw_im[t_chunk, 0], pw_im[t_chunk, 1]], axis=-1)[:, None, :]
    return w_in.astype(BF16), m.astype(BF16), w_out.astype(BF16), at_re, at_im


def _ssm_kernel(uc_ref, ul_ref, win_ref, m_ref, wout_ref, are_ref, aim_ref, y_ref, s_ref, xin_ref,
                *, nb, n_ctx, n_lat):
    n_all = n_ctx + n_lat
    lat = n_ctx * nb
    half = s_ref.shape[1] // 2
    quarter = half // 2
    u_ctx = uc_ref[0].reshape(lat, uc_ref.shape[3])
    u_lat = ul_ref[0].reshape(n_lat * nb, ul_ref.shape[3])
    s_ref[0:lat, :] = jnp.dot(u_ctx, win_ref[0], preferred_element_type=F32)
    s_ref[lat:, :] = jnp.dot(u_lat, win_ref[0], preferred_element_type=F32)
    a_re = are_ref[0]
    a_im = aim_ref[0]
    is_f = lax.broadcasted_iota(jnp.int32, (nb, half), 1) < quarter

    def step(k, carry):
        x_re, x_im = carry
        cb = jnp.where(k < n_ctx, n_ctx - 1 - k, n_all + n_ctx - 1 - k)
        rf = pl.multiple_of(k * nb, nb)
        rb = pl.multiple_of(cb * nb, nb)
        xin_ref[pl.ds(rf, nb), 0:quarter] = x_re[:, 0:quarter]
        xin_ref[pl.ds(rb, nb), quarter:half] = x_re[:, quarter:half]
        xin_ref[pl.ds(rf, nb), half:half + quarter] = x_im[:, 0:quarter]
        xin_ref[pl.ds(rb, nb), half + quarter:2 * half] = x_im[:, quarter:half]
        s_f = s_ref[pl.ds(rf, nb), :]
        s_b = s_ref[pl.ds(rb, nb), :]
        s_re = jnp.where(is_f, s_f[:, :half], s_b[:, :half])
        s_im = jnp.where(is_f, s_f[:, half:], s_b[:, half:])
        return (a_re * x_re - a_im * x_im + s_re, a_re * x_im + a_im * x_re + s_im)

    zero = jnp.zeros((nb, half), F32)
    lax.fori_loop(0, n_all, step, (zero, zero))

    y = jnp.dot(u_lat, m_ref[0], preferred_element_type=F32)
    y = y + jnp.dot(xin_ref[lat:, :].astype(BF16), wout_ref[0], preferred_element_type=F32)
    y_ref[0] = y.astype(BF16).reshape(y_ref.shape[1:])


def _ssm(u_ctx, u_lat, w_in, m, w_out, at_re, at_im):
    n_g, n_lat, nb, th = u_lat.shape
    n_ctx = u_ctx.shape[1]
    rows = (n_ctx + n_lat) * nb
    ns = w_in.shape[2]
    per_g = lambda a: pl.BlockSpec((1,) + a.shape[1:], lambda g: (g,) + (0,) * (a.ndim - 1))
    ins = (u_ctx, u_lat, w_in, m, w_out, at_re, at_im)
    return pl.pallas_call(
        functools.partial(_ssm_kernel, nb=nb, n_ctx=n_ctx, n_lat=n_lat),
        grid=(n_g,),
        in_specs=[per_g(a) for a in ins],
        out_specs=per_g(u_lat),
        out_shape=jax.ShapeDtypeStruct(u_lat.shape, BF16),
        scratch_shapes=[pltpu.VMEM((rows, ns), F32), pltpu.VMEM((rows, ns), F32)],
        compiler_params=_params("parallel"),
        name="ssm",
    )(*ins)


def _merge_kernel(ya_ref, bg_ref, cx_ref, cxp_ref, cxn_ref, ga_ref, gb_ref, x_ref,
                  g1_ref, sh2_ref, sc2_ref, gluw_ref, glub_ref, scw_ref, pa_ref, pb_ref, wo_ref, n2_ref,
                  x1_ref, h2_ref):
    i = pl.program_id(1)
    tm = cx_ref.shape[1]
    z = jax.nn.gelu(ya_ref[0].astype(F32), approximate=True)
    gate = jax.nn.sigmoid(jnp.dot(z.astype(BF16), gluw_ref[...], preferred_element_type=F32) + glub_ref[...])
    y_a = (z * gate).astype(BF16)

    cx = cx_ref[0].astype(F32)
    prev = cxp_ref[0].astype(F32)[BF16_SUBLANES - 1:BF16_SUBLANES]
    nxt = cxn_ref[0].astype(F32)[0:1]
    prev = jnp.where(i > 0, prev, 0.0)
    nxt = jnp.where(i < pl.num_programs(1) - 1, nxt, 0.0)
    row = lax.broadcasted_iota(jnp.int32, (tm, 1), 0)
    c_m1 = jnp.where(row == 0, prev, pltpu.roll(cx, 1, 0))
    c_p1 = jnp.where(row == tm - 1, nxt, pltpu.roll(cx, tm - 1, 0))
    w = scw_ref[...]
    y_b = (bg_ref[0].astype(F32) * (w[0:1] * c_m1 + w[1:2] * cx + w[2:3] * c_p1)).astype(BF16)

    merged = (ga_ref[0].astype(F32) * jnp.dot(y_a, pa_ref[...], preferred_element_type=F32)
              + gb_ref[0].astype(F32) * jnp.dot(y_b, pb_ref[...], preferred_element_type=F32))
    x1 = x_ref[0] + g1_ref[0] * jnp.dot(merged.astype(BF16), wo_ref[...], preferred_element_type=F32)
    x1_ref[0] = x1
    h2_ref[0] = ((x1 * _rms(x1)) * n2_ref[...] * (1.0 + sc2_ref[0]) + sh2_ref[0]).astype(BF16)


def _merge(ya, bg, cx, ga, gb, x, mods3, glu_w, glu_b, sconv_w, proj_a, proj_b, w_out, n2, tm):
    bsz, seq, d = x.shape
    sw, cw = ya.shape[2], cx.shape[2]
    hb = BF16_SUBLANES
    last_hb = seq // hb - 1
    tok = lambda n: pl.BlockSpec((1, tm, n), lambda b, i: (b, i, 0))
    mod = lambda k: pl.BlockSpec((1, 1, d), lambda b, i: (b, 0, k))
    full = lambda a: pl.BlockSpec(a.shape, lambda b, i: (0,) * a.ndim)
    prev_spec = pl.BlockSpec((1, hb, cw), lambda b, i: (b, jnp.maximum(i * (tm // hb) - 1, 0), 0))
    next_spec = pl.BlockSpec((1, hb, cw), lambda b, i: (b, jnp.minimum((i + 1) * (tm // hb), last_hb), 0))
    consts = (glu_w, glu_b, sconv_w, proj_a, proj_b, w_out, n2)
    return pl.pallas_call(
        _merge_kernel,
        grid=(bsz, seq // tm),
        in_specs=[tok(sw), tok(cw), tok(cw), prev_spec, next_spec, tok(d), tok(d), tok(d),
                  mod(2), mod(3), mod(4)] + [full(a) for a in consts],
        out_specs=[tok(d), tok(d)],
        out_shape=[jax.ShapeDtypeStruct((bsz, seq, d), F32), jax.ShapeDtypeStruct((bsz, seq, d), BF16)],
        compiler_params=_params("parallel", "parallel"),
        name="merge",
    )(ya, bg, cx, cx, cx, ga, gb, x, mods3, mods3, mods3, *consts)


FFN_ROW_BLOCK = 256


def _ffn_kernel(h_ref, x1_ref, g2_ref, fg_ref, wa_ref, wv_ref, cwa_ref, cwv_ref, wd_ref, o_ref,
                upa0_ref, upv0_ref, upa1_ref, upv1_ref, act0_ref, act1_ref, *, gw, lag):
    b, j = pl.program_id(0), pl.program_id(1)
    n_b, nj = pl.num_programs(0) - 1, pl.num_programs(1)
    seq, ck = act0_ref.shape
    d = o_ref.shape[2]
    rb = min(FFN_ROW_BLOCK, seq)
    active = jnp.logical_or(b < n_b, j < 2)
    filled = jnp.logical_or(b > 0, j >= 2)
    j2 = lag(b, j, 2)[1]
    odd = (b * nj + j) % 2 == 1

    @pl.when(jnp.logical_and(b == 0, j == 0))
    def _():
        for ref in (upa0_ref, upv0_ref, upa1_ref, upv1_ref, act0_ref, act1_ref):
            ref[...] = jnp.zeros(ref.shape, BF16)

    @pl.when(jnp.logical_and(active, j2 == 0))
    def _():
        o_ref[0] = jnp.zeros(o_ref.shape[1:], F32)

    col = lax.broadcasted_iota(jnp.int32, (gw, ck), 0)
    first_col = col == 0
    last_col = col == gw - 1

    def stages(up_w, up_r, act_w, act_r):
        def up_piece(lo, w_ref, dst):
            def run():
                u = jnp.dot(h_ref[0, lo:lo + rb, :], w_ref[...], preferred_element_type=F32)
                for g0 in range(0, rb, gw):
                    ug = u[g0:g0 + gw, :]
                    r = gw + lo + g0
                    dst[0, r:r + gw, :] = jnp.where(first_col, 0.0, pltpu.roll(ug, 1, 0)).astype(BF16)
                    dst[1, r:r + gw, :] = ug.astype(BF16)
                    dst[2, r:r + gw, :] = jnp.where(last_col, 0.0, pltpu.roll(ug, gw - 1, 0)).astype(BF16)
            return run

        def down_piece(lo):
            def run():
                o_ref[0, lo:lo + rb, :] += jnp.dot(act_r[lo:lo + rb, :], wd_ref[...], preferred_element_type=F32)
            return run

        cw = [cwa_ref[...].astype(BF16), cwv_ref[...].astype(BF16)]

        def conv_piece(base, l0):
            def run():
                acc = []
                for src, w in zip(up_r, cw):
                    t = None
                    for dy in range(3):
                        for dx in range(3):
                            k = dy * 3 + dx
                            term = w[k:k + 1, l0:l0 + LANES] * src[dx, base + dy * gw:base + (dy + 1) * gw,
                                                                   l0:l0 + LANES]
                            t = term if t is None else t + term
                    acc.append(t)
                a, v = acc
                act_w[base:base + gw, l0:l0 + LANES] = a * jax.nn.sigmoid(a) * v
            return run

        mxu = []
        for lo in range(0, seq, rb):
            mxu += [(d, up_piece(lo, wa_ref, up_w[0])), (d, up_piece(lo, wv_ref, up_w[1])), (d, down_piece(lo))]
        vpu = [conv_piece(base, l0) for base in range(0, seq, gw) for l0 in range(0, ck, LANES)]
        total = sum(c for c, _ in mxu)
        done, k = 0, 0
        for c, piece in mxu:
            piece()
            done += c
            while k < len(vpu) and (k + 1) * total <= done * len(vpu):
                vpu[k]()
                k += 1
        for piece in vpu[k:]:
            piece()

    @pl.when(jnp.logical_and(active, jnp.logical_not(odd)))
    def _():
        stages((upa0_ref, upv0_ref), (upa1_ref, upv1_ref), act1_ref, act0_ref)

    @pl.when(jnp.logical_and(active, odd))
    def _():
        stages((upa1_ref, upv1_ref), (upa0_ref, upv0_ref), act0_ref, act1_ref)

    @pl.when(jnp.logical_and(jnp.logical_and(active, filled), j2 == nj - 1))
    def _():
        x2 = x1_ref[0] + g2_ref[0] * o_ref[0]
        o_ref[0] = (x2 * _rms(x2)) * fg_ref[...]


def _ffn(h2, x1, mods3, final_g, w_up, conv_w, w_down, ck):
    bsz, seq, d = x1.shape
    hid = w_down.shape[0]
    nj = hid // ck
    taps = conv_w.shape[0]
    assert nj >= 2

    def lag(b, j, k):
        jj = j - k
        borrow = jj < 0
        bb = jnp.where(borrow, b - 1, b)
        jj = jnp.where(borrow, jj + nj, jj)
        before = bb < 0
        after = bb >= bsz
        bb = jnp.where(before, 0, jnp.where(after, bsz - 1, bb))
        jj = jnp.where(before, 0, jnp.where(after, nj - 1, jj))
        return bb, jj

    return pl.pallas_call(
        functools.partial(_ffn_kernel, gw=GRID_W, lag=lag),
        grid=(bsz + 1, nj),
        in_specs=[pl.BlockSpec((1, seq, d), lambda b, j: (lag(b, j, 0)[0], 0, 0)),
                  pl.BlockSpec((1, seq, d), lambda b, j: (lag(b, j, 2)[0], 0, 0), pipeline_mode=pl.Buffered(1)),
                  pl.BlockSpec((1, 1, d), lambda b, j: (lag(b, j, 2)[0], 0, 5)),
                  pl.BlockSpec((1, d), lambda b, j: (0, 0)),
                  pl.BlockSpec((d, ck), lambda b, j: (0, lag(b, j, 0)[1])),
                  pl.BlockSpec((d, ck), lambda b, j: (0, nj + lag(b, j, 0)[1])),
                  pl.BlockSpec((taps, ck), lambda b, j: (0, lag(b, j, 1)[1])),
                  pl.BlockSpec((taps, ck), lambda b, j: (0, nj + lag(b, j, 1)[1])),
                  pl.BlockSpec((ck, d), lambda b, j: (lag(b, j, 2)[1], 0))],
        out_specs=pl.BlockSpec((1, seq, d), lambda b, j: (lag(b, j, 2)[0], 0, 0)),
        out_shape=jax.ShapeDtypeStruct((bsz, seq, d), F32),
        scratch_shapes=[pltpu.VMEM((3, seq + 2 * GRID_W, ck), BF16)] * 4 + [pltpu.VMEM((seq, ck), BF16)] * 2,
        compiler_params=_params("arbitrary", "arbitrary"),
        name="ffn",
    )(h2, x1, mods3, final_g, w_up, w_up, conv_w, conv_w, w_down)


def _layer(x, ctx, mods3, ctx_row, norm1_g, norm2_g, w_in, ssm, glu_w, glu_b, sconv_w,
           proj_a, proj_b, w_out, ffn_w_up, ffn_conv_w, ffn_w_down, out_g):
    bsz, seq, d = x.shape
    ctx_len = ctx.shape[1]
    n_g, th = ssm[1].shape[0], ssm[1].shape[1]
    sw = glu_w.shape[0]
    cw = sconv_w.shape[1]
    n_h = sw // n_g
    t_chunk = th // n_h
    n_ctx, n_lat = ctx_len // t_chunk, seq // t_chunk
    tm = min(512, seq)
    w_in_b = w_in.astype(BF16)

    u, bg, cx, ga, gb = _inproj(x, mods3, norm1_g, w_in_b, sw, cw, tm)
    u_ctx = _ctxproj(ctx, mods3, ctx_row, norm1_g, w_in_b[:, :sw], min(tm, ctx_len))
    ya = _from_chunks(_ssm(_to_chunks(u_ctx, n_g), _to_chunks(u, n_g), *ssm))

    x1, h2 = _merge(ya, bg, cx, ga, gb, x, mods3, glu_w.astype(BF16), glu_b.reshape(1, sw), sconv_w,
                    proj_a.astype(BF16), proj_b.astype(BF16), w_out.astype(BF16), norm2_g, tm)
    k2 = ffn_conv_w.shape[0] * ffn_conv_w.shape[1]
    return _ffn(h2, x1, mods3, out_g, ffn_w_up.astype(BF16), ffn_conv_w.reshape(k2, -1),
                ffn_w_down.astype(BF16), 256)


def kernel(x, c, ctx, c_ctx, mod_w, mod_b, norm1_g, norm2_g, w_in, ssm_lambda_re, ssm_lambda_im,
           ssm_log_dt, ssm_b_re, ssm_b_im, ssm_c_re, ssm_c_im, ssm_d, ssm_glu_w, ssm_glu_b,
           sconv_w, proj_a, proj_b, w_out, ffn_w_up, ffn_conv_w, ffn_w_down, final_g):
    depth = mod_w.shape[0]
    assert depth == 1, "context-stream update between layers is not implemented"
    bsz, seq, d = x.shape
    assert seq % GRID_W == 0 and seq % SSM_CHUNK == 0 and ctx.shape[1] % SSM_CHUNK == 0
    i = 0
    rows = -(-(bsz + 1) // 8) * 8
    cond = jnp.zeros((rows, d), F32).at[:bsz].set(c).at[bsz].set(c_ctx)
    mods3 = _adaln(cond, mod_w[i], mod_b[i]).reshape(rows, 1, N_MOD * d)
    ssm = _ssm_weights(ssm_lambda_re[i], ssm_lambda_im[i], ssm_log_dt[i], ssm_b_re[i], ssm_b_im[i],
                       ssm_c_re[i], ssm_c_im[i], ssm_d[i], SSM_CHUNK)
    return _layer(x, ctx, mods3, bsz, norm1_g[i].reshape(1, d), norm2_g[i].reshape(1, d), w_in[i], ssm,
                  ssm_glu_w[i], ssm_glu_b[i], sconv_w[i], proj_a[i], proj_b[i], w_out[i],
                  ffn_w_up[i], ffn_conv_w[i], ffn_w_down[i], final_g.reshape(1, d))
```

```python
import functools

import jax
import jax.numpy as jnp
from jax import lax
from jax.experimental import pallas as pl
from jax.experimental.pallas import tpu as pltpu

F32 = jnp.float32
BF16 = jnp.bfloat16

EPS = 1e-6
GRID_W = 64
N_MOD = 6
SSM_CHUNK = 16
SUBLANES = 8
LANES = 128
BF16_SUBLANES = 16
MXU_TILE = 256
VMEM_LIMIT_BYTES = 56 * 1024 * 1024
HIGHEST = lax.Precision.HIGHEST


def _params(*semantics):
    return pltpu.CompilerParams(dimension_semantics=semantics, vmem_limit_bytes=VMEM_LIMIT_BYTES)


def _rms(x):
    return lax.rsqrt(jnp.mean(x * x, axis=-1, keepdims=True) + EPS)


def _adaln_kernel(c_ref, w_ref, b_ref, o_ref):
    s = c_ref[...]
    s = s * jax.nn.sigmoid(s)
    o_ref[...] = jnp.dot(s, w_ref[...], preferred_element_type=F32, precision=HIGHEST) + b_ref[...]


def _adaln(cond, w, b):
    r, d = cond.shape
    n = w.shape[1]
    tn = d
    return pl.pallas_call(
        _adaln_kernel,
        grid=(n // tn,),
        in_specs=[pl.BlockSpec((r, d), lambda j: (0, 0)),
                  pl.BlockSpec((d, tn), lambda j: (0, j)),
                  pl.BlockSpec((1, tn), lambda j: (0, j))],
        out_specs=pl.BlockSpec((r, tn), lambda j: (0, j)),
        out_shape=jax.ShapeDtypeStruct((r, n), F32),
        compiler_params=_params("parallel"),
        name="adaln",
    )(cond, w, b.reshape(1, n))


def _norm_mod(x_ref, sh_ref, sc_ref, g_ref):
    x = x_ref[0]
    return ((x * _rms(x)) * g_ref[...] * (1.0 + sc_ref[0]) + sh_ref[0]).astype(BF16)


def _inproj_kernel(x_ref, sh_ref, sc_ref, g_ref, w_ref, u_ref, bg_ref, cx_ref, ga_ref, gb_ref,
                   *, sw, cw, d):
    hb = _norm_mod(x_ref, sh_ref, sc_ref, g_ref)

    def proj(lo, n):
        return jnp.dot(hb, w_ref[:, lo:lo + n], preferred_element_type=F32)

    u_ref[0] = proj(0, sw).astype(BF16)
    bg_ref[0] = proj(sw, cw).astype(BF16)
    cx_ref[0] = (proj(sw + cw, cw) * proj(sw + 2 * cw, cw)).astype(BF16)
    o2 = sw + 3 * cw
    ga_ref[0] = jax.nn.sigmoid(proj(o2, d)).astype(BF16)
    gb_ref[0] = jax.nn.sigmoid(proj(o2 + d, d)).astype(BF16)


def _inproj(x, mods3, g, w, sw, cw, tm):
    bsz, seq, d = x.shape
    cols = w.shape[1]
    tok = lambda n: pl.BlockSpec((1, tm, n), lambda b, i: (b, i, 0))
    mod = lambda k: pl.BlockSpec((1, 1, d), lambda b, i: (b, 0, k))
    out = lambda n: jax.ShapeDtypeStruct((bsz, seq, n), BF16)
    return pl.pallas_call(
        functools.partial(_inproj_kernel, sw=sw, cw=cw, d=d),
        grid=(bsz, seq // tm),
        in_specs=[tok(d), mod(0), mod(1),
                  pl.BlockSpec((1, d), lambda b, i: (0, 0)),
                  pl.BlockSpec((d, cols), lambda b, i: (0, 0))],
        out_specs=[tok(sw), tok(cw), tok(cw), tok(d), tok(d)],
        out_shape=[out(sw), out(cw), out(cw), out(d), out(d)],
        compiler_params=_params("parallel", "parallel"),
        name="inproj",
    )(x, mods3, mods3, g, w)


def _ctxproj_kernel(x_ref, sh_ref, sc_ref, g_ref, w_ref, u_ref):
    hb = _norm_mod(x_ref, sh_ref, sc_ref, g_ref)
    u_ref[0] = jnp.dot(hb, w_ref[...], preferred_element_type=F32).astype(BF16)


def _ctxproj(ctx, mods3, row, g, w, tm):
    bsz, seq, d = ctx.shape
    sw = w.shape[1]
    mod = lambda k: pl.BlockSpec((1, 1, d), lambda b, i: (row, 0, k))
    return pl.pallas_call(
        _ctxproj_kernel,
        grid=(bsz, seq // tm),
        in_specs=[pl.BlockSpec((1, tm, d), lambda b, i: (b, i, 0)), mod(0), mod(1),
                  pl.BlockSpec((1, d), lambda b, i: (0, 0)),
                  pl.BlockSpec((d, sw), lambda b, i: (0, 0))],
        out_specs=pl.BlockSpec((1, tm, sw), lambda b, i: (b, i, 0)),
        out_shape=jax.ShapeDtypeStruct((bsz, seq, sw), BF16),
        compiler_params=_params("parallel", "parallel"),
        name="ctxproj",
    )(ctx, mods3, mods3, g, w)


RELAYOUT_BATCH = BF16_SUBLANES
RELAYOUT_CHUNKS = 4
RELAYOUT_TILES_IN_FLIGHT = 4


def _block_transpose(groups, width):
    n = len(groups[0])
    shape = groups[0][0].shape
    lanes = shape[-1]
    block = lax.broadcasted_iota(jnp.int32, shape, len(shape) - 1) // width
    s = n // 2
    while s:
        keep = (block & s) == 0
        nxt = []
        for regs in groups:
            new = list(regs)
            for i in range(n):
                if i & s == 0:
                    a, b = regs[i], regs[i | s]
                    new[i] = jnp.where(keep, a, pltpu.roll(b, s * width, 1))
                    new[i | s] = jnp.where(keep, pltpu.roll(a, lanes - s * width, 1), b)
            nxt.append(new)
        groups = nxt
        s //= 2
    return groups


def _chunk_tiles(ntok, width, n_h, t_chunk):
    per = LANES // n_h
    n_q = width // LANES
    step = min(RELAYOUT_TILES_IN_FLIGHT, n_q)
    for cl in range(ntok // t_chunk):
        for hf in range(t_chunk // per):
            for q0 in range(0, n_q, step):
                yield cl, hf, range(q0, q0 + step), per, cl * t_chunk + hf * per


def _to_chunks_kernel(u_ref, o_ref, s_ref, *, n_h):
    nb, ntok, width = u_ref.shape
    t_chunk = o_ref.shape[3] // n_h
    pitch = s_ref.shape[1] // nb
    for q in range(width // LANES):
        for b in range(nb):
            s_ref[q, b * pitch:b * pitch + ntok, :] = u_ref[b, :, q * LANES:(q + 1) * LANES].astype(F32)
    for cl, hf, qs, per, tok in _chunk_tiles(ntok, width, n_h, t_chunk):
        groups = [[s_ref[q, pl.ds(tok + j, nb, stride=pitch), :] for j in range(per)] for q in qs]
        for q, regs in zip(qs, _block_transpose(groups, n_h)):
            for gl, r in enumerate(regs):
                o_ref[q * per + gl, cl, :, hf * LANES:(hf + 1) * LANES] = r.astype(BF16)


def _from_chunks_kernel(y_ref, o_ref, s_ref, *, n_h):
    nb, ntok, width = o_ref.shape
    t_chunk = y_ref.shape[3] // n_h
    pitch = s_ref.shape[1] // nb
    for cl, hf, qs, per, tok in _chunk_tiles(ntok, width, n_h, t_chunk):
        groups = [[y_ref[q * per + gl, cl, :, hf * LANES:(hf + 1) * LANES].astype(F32) for gl in range(per)]
                  for q in qs]
        for q, regs in zip(qs, _block_transpose(groups, n_h)):
            for j, r in enumerate(regs):
                s_ref[q, pl.ds(tok + j, nb, stride=pitch), :] = r
    for q in range(width // LANES):
        for b in range(nb):
            o_ref[b, :, q * LANES:(q + 1) * LANES] = s_ref[q, b * pitch:b * pitch + ntok, :].astype(BF16)


def _chunk_specs(bsz, seq, width, n_g):
    n_h = width // n_g
    nb = min(RELAYOUT_BATCH, bsz)
    ntok = RELAYOUT_CHUNKS * SSM_CHUNK
    assert bsz % nb == 0 and seq % ntok == 0 and LANES % n_h == 0 and width % LANES == 0
    nat = pl.BlockSpec((nb, ntok, width), lambda bi, ti: (bi, ti, 0))
    chunked = pl.BlockSpec((n_g, RELAYOUT_CHUNKS, nb, SSM_CHUNK * n_h), lambda bi, ti: (0, ti, bi, 0))
    pitch = ntok + SUBLANES if (ntok // SUBLANES) % 2 == 0 else ntok
    scratch = pltpu.VMEM((width // LANES, nb * pitch, LANES), F32)
    return dict(grid=(bsz // nb, seq // ntok), scratch_shapes=[scratch],
                compiler_params=_params("parallel", "parallel")), nat, chunked, n_h


def _to_chunks(u, n_g):
    bsz, seq, width = u.shape
    common, nat, chunked, n_h = _chunk_specs(bsz, seq, width, n_g)
    return pl.pallas_call(
        functools.partial(_to_chunks_kernel, n_h=n_h), in_specs=[nat], out_specs=chunked,
        out_shape=jax.ShapeDtypeStruct((n_g, seq // SSM_CHUNK, bsz, SSM_CHUNK * n_h), BF16),
        name="to_chunks", **common)(u)


def _from_chunks(y):
    n_g, n_chunks, bsz, th = y.shape
    n_h = th // SSM_CHUNK
    seq = n_chunks * SSM_CHUNK
    common, nat, chunked, n_h = _chunk_specs(bsz, seq, n_g * n_h, n_g)
    return pl.pallas_call(
        functools.partial(_from_chunks_kernel, n_h=n_h), in_specs=[chunked], out_specs=nat,
        out_shape=jax.ShapeDtypeStruct((bsz, seq, n_g * n_h), BF16),
        name="from_chunks", **common)(y)


def _lag_kernel(e_ref, b_ref, o_ref):
    for dr in range(e_ref.shape[0]):
        o_ref[dr, 0] = jnp.dot(e_ref[dr, 0], b_ref[dr, 0], preferred_element_type=F32, precision=HIGHEST)


def _lag_kernels(lhs, rhs):
    n_dir, n_g, rows, k = lhs.shape
    n_h = rhs.shape[3]
    spec = lambda r, c: pl.BlockSpec((n_dir, 1, r, c), lambda g: (0, g, 0, 0))
    return pl.pallas_call(
        _lag_kernel,
        grid=(n_g,),
        in_specs=[spec(rows, k), spec(k, n_h)],
        out_specs=spec(rows, n_h),
        out_shape=jax.ShapeDtypeStruct((n_dir, n_g, rows, n_h), F32),
        compiler_params=_params("parallel"),
        name="lag_kernels",
    )(lhs, rhs)


def _ssm_weights(lam_re, lam_im, log_dt, b_re, b_im, c_re, c_im, d_skip, t_chunk):
    n_dir, n_g, n_p = lam_re.shape
    n_h = b_re.shape[-1]
    dt = jnp.exp(log_dt)[..., None]
    xr, xi = lam_re * dt, lam_im * dt
    mag = jnp.exp(xr)
    a_re, a_im = mag * jnp.cos(xi), mag * jnp.sin(xi)
    n_re = jnp.expm1(xr) * jnp.cos(xi) - 2.0 * jnp.sin(0.5 * xi) ** 2
    n_im = a_im
    den = lam_re * lam_re + lam_im * lam_im
    q_re = (n_re * lam_re + n_im * lam_im) / den
    q_im = (n_im * lam_re - n_re * lam_im) / den
    bb_re = q_re[..., None] * b_re - q_im[..., None] * b_im
    bb_im = q_re[..., None] * b_im + q_im[..., None] * b_re

    pw_re, pw_im = [jnp.ones_like(a_re)], [jnp.zeros_like(a_im)]
    for _ in range(t_chunk):
        pr, pi = pw_re[-1], pw_im[-1]
        pw_re.append(pr * a_re - pi * a_im)
        pw_im.append(pr * a_im + pi * a_re)
    pw_re, pw_im = jnp.stack(pw_re), jnp.stack(pw_im)

    def by_token(pw, lo, flip_dir):
        p = pw[lo:lo + t_chunk]
        return jnp.stack([p[::-1, d] if d == flip_dir else p[:, d] for d in range(n_dir)], axis=1)

    pin_re, pin_im = by_token(pw_re, 0, 0), by_token(pw_im, 0, 0)
    win_re = pin_re[..., None] * bb_re[None] - pin_im[..., None] * bb_im[None]
    win_im = pin_re[..., None] * bb_im[None] + pin_im[..., None] * bb_re[None]

    def to_in(w, k):
        return jnp.transpose(w[:, k], (1, 0, 3, 2)).reshape(n_g, t_chunk * n_h, n_p)

    w_in = jnp.concatenate([to_in(win_re, 0), to_in(win_re, 1), to_in(win_im, 0), to_in(win_im, 1)], axis=-1)

    pout_re, pout_im = by_token(pw_re, 1, 1), by_token(pw_im, 1, 1)
    cr, ci = c_re[None], c_im[None]
    e_re = cr * pout_re[:, :, :, None, :] - ci * pout_im[:, :, :, None, :]
    e_im = cr * pout_im[:, :, :, None, :] + ci * pout_re[:, :, :, None, :]

    def to_out(w, k):
        return jnp.transpose(w[:, k], (1, 3, 0, 2)).reshape(n_g, n_p, t_chunk * n_h)

    w_out = jnp.concatenate([to_out(e_re, 0), to_out(e_re, 1), to_out(-e_im, 0), to_out(-e_im, 1)], axis=1)

    ca_re = cr * pw_re[:t_chunk, :, :, None, :] - ci * pw_im[:t_chunk, :, :, None, :]
    ca_im = cr * pw_im[:t_chunk, :, :, None, :] + ci * pw_re[:t_chunk, :, :, None, :]
    lhs = jnp.concatenate([ca_re, -ca_im], axis=-1)
    lhs = jnp.transpose(lhs, (1, 2, 0, 3, 4)).reshape(n_dir, n_g, t_chunk * n_h, 2 * n_p)
    kk = _lag_kernels(lhs, jnp.concatenate([bb_re, bb_im], axis=2))
    kk_t = jnp.transpose(kk.reshape(n_dir, n_g, t_chunk, n_h, n_h), (0, 1, 4, 2, 3))
    k_0 = kk_t[0, :, :, 0] + kk_t[1, :, :, 0] + jnp.eye(n_h, dtype=F32)[None] * d_skip.reshape(n_g, n_h, 1)
    strip = jnp.concatenate([kk_t[1, :, :, :0:-1].reshape(n_g, n_h, -1), k_0,
                             kk_t[0, :, :, 1:].reshape(n_g, n_h, -1)], axis=-1)
    th = t_chunk * n_h
    m = jnp.stack([strip[:, :, (t_chunk - 1 - t) * n_h:(t_chunk - 1 - t) * n_h + th] for t in range(t_chunk)],
                  axis=1).reshape(n_g, th, th)

    at_re = jnp.concatenate([pw_re[t_chunk, 0], pw_re[t_chunk, 1]], axis=-1)[:, None, :]
    at_im = jnp.concatenate([pw_im[t_chunk, 0], pw_im[t_chunk, 1]], axis=-1)[:, None, :]
    return w_in.astype(BF16), m.astype(BF16), w_out.astype(BF16), at_re, at_im


def _ssm_kernel(uc_ref, ul_ref, win_ref, m_ref, wout_ref, are_ref, aim_ref, y_ref, s_ref, xin_ref,
                *, nb, n_ctx, n_lat):
    n_all = n_ctx + n_lat
    lat = n_ctx * nb
    half = s_ref.shape[1] // 2
    quarter = half // 2
    u_ctx = uc_ref[0].reshape(lat, uc_ref.shape[3])
    u_lat = ul_ref[0].reshape(n_lat * nb, ul_ref.shape[3])
    s_ref[0:lat, :] = jnp.dot(u_ctx, win_ref[0], preferred_element_type=F32)
    s_ref[lat:, :] = jnp.dot(u_lat, win_ref[0], preferred_element_type=F32)
    a_re = are_ref[0]
    a_im = aim_ref[0]
    is_f = lax.broadcasted_iota(jnp.int32, (nb, half), 1) < quarter

    def step(k, carry):
        x_re, x_im = carry
        cb = jnp.where(k < n_ctx, n_ctx - 1 - k, n_all + n_ctx - 1 - k)
        rf = pl.multiple_of(k * nb, nb)
        rb = pl.multiple_of(cb * nb, nb)
        xin_ref[pl.ds(rf, nb), 0:quarter] = x_re[:, 0:quarter]
        xin_ref[pl.ds(rb, nb), quarter:half] = x_re[:, quarter:half]
        xin_ref[pl.ds(rf, nb), half:half + quarter] = x_im[:, 0:quarter]
        xin_ref[pl.ds(rb, nb), half + quarter:2 * half] = x_im[:, quarter:half]
        s_f = s_ref[pl.ds(rf, nb), :]
        s_b = s_ref[pl.ds(rb, nb), :]
        s_re = jnp.where(is_f, s_f[:, :half], s_b[:, :half])
        s_im = jnp.where(is_f, s_f[:, half:], s_b[:, half:])
        return (a_re * x_re - a_im * x_im + s_re, a_re * x_im + a_im * x_re + s_im)

    zero = jnp.zeros((nb, half), F32)
    lax.fori_loop(0, n_all, step, (zero, zero))

    y = jnp.dot(u_lat, m_ref[0], preferred_element_type=F32)
    y = y + jnp.dot(xin_ref[lat:, :].astype(BF16), wout_ref[0], preferred_element_type=F32)
    y_ref[0] = y.astype(BF16).reshape(y_ref.shape[1:])


def _ssm(u_ctx, u_lat, w_in, m, w_out, at_re, at_im):
    n_g, n_lat, nb, th = u_lat.shape
    n_ctx = u_ctx.shape[1]
    rows = (n_ctx + n_lat) * nb
    ns = w_in.shape[2]
    per_g = lambda a: pl.BlockSpec((1,) + a.shape[1:], lambda g: (g,) + (0,) * (a.ndim - 1))
    ins = (u_ctx, u_lat, w_in, m, w_out, at_re, at_im)
    return pl.pallas_call(
        functools.partial(_ssm_kernel, nb=nb, n_ctx=n_ctx, n_lat=n_lat),
        grid=(n_g,),
        in_specs=[per_g(a) for a in ins],
        out_specs=per_g(u_lat),
        out_shape=jax.ShapeDtypeStruct(u_lat.shape, BF16),
        scratch_shapes=[pltpu.VMEM((rows, ns), F32), pltpu.VMEM((rows, ns), F32)],
        compiler_params=_params("parallel"),
        name="ssm",
    )(*ins)


def _merge_kernel(ya_ref, bg_ref, cx_ref, cxp_ref, cxn_ref, ga_ref, gb_ref, x_ref,
                  g1_ref, sh2_ref, sc2_ref, gluw_ref, glub_ref, scw_ref, pa_ref, pb_ref, wo_ref, n2_ref,
                  x1_ref, h2_ref):
    i = pl.program_id(1)
    tm = cx_ref.shape[1]
    z = jax.nn.gelu(ya_ref[0].astype(F32), approximate=True)
    gate = jax.nn.sigmoid(jnp.dot(z.astype(BF16), gluw_ref[...], preferred_element_type=F32) + glub_ref[...])
    y_a = (z * gate).astype(BF16)

    cx = cx_ref[0].astype(F32)
    prev = cxp_ref[0].astype(F32)[BF16_SUBLANES - 1:BF16_SUBLANES]
    nxt = cxn_ref[0].astype(F32)[0:1]
    prev = jnp.where(i > 0, prev, 0.0)
    nxt = jnp.where(i < pl.num_programs(1) - 1, nxt, 0.0)
    row = lax.broadcasted_iota(jnp.int32, (tm, 1), 0)
    c_m1 = jnp.where(row == 0, prev, pltpu.roll(cx, 1, 0))
    c_p1 = jnp.where(row == tm - 1, nxt, pltpu.roll(cx, tm - 1, 0))
    w = scw_ref[...]
    y_b = (bg_ref[0].astype(F32) * (w[0:1] * c_m1 + w[1:2] * cx + w[2:3] * c_p1)).astype(BF16)

    merged = (ga_ref[0].astype(F32) * jnp.dot(y_a, pa_ref[...], preferred_element_type=F32)
              + gb_ref[0].astype(F32) * jnp.dot(y_b, pb_ref[...], preferred_element_type=F32))
    x1 = x_ref[0] + g1_ref[0] * jnp.dot(merged.astype(BF16), wo_ref[...], preferred_element_type=F32)
    x1_ref[0] = x1
    h2_ref[0] = ((x1 * _rms(x1)) * n2_ref[...] * (1.0 + sc2_ref[0]) + sh2_ref[0]).astype(BF16)


def _merge(ya, bg, cx, ga, gb, x, mods3, glu_w, glu_b, sconv_w, proj_a, proj_b, w_out, n2, tm):
    bsz, seq, d = x.shape
    sw, cw = ya.shape[2], cx.shape[2]
    hb = BF16_SUBLANES
    last_hb = seq // hb - 1
    tok = lambda n: pl.BlockSpec((1, tm, n), lambda b, i: (b, i, 0))
    mod = lambda k: pl.BlockSpec((1, 1, d), lambda b, i: (b, 0, k))
    full = lambda a: pl.BlockSpec(a.shape, lambda b, i: (0,) * a.ndim)
    prev_spec = pl.BlockSpec((1, hb, cw), lambda b, i: (b, jnp.maximum(i * (tm // hb) - 1, 0), 0))
    next_spec = pl.BlockSpec((1, hb, cw), lambda b, i: (b, jnp.minimum((i + 1) * (tm // hb), last_hb), 0))
    consts = (glu_w, glu_b, sconv_w, proj_a, proj_b, w_out, n2)
    return pl.pallas_call(
        _merge_kernel,
        grid=(bsz, seq // tm),
        in_specs=[tok(sw), tok(cw), tok(cw), prev_spec, next_spec, tok(d), tok(d), tok(d),
                  mod(2), mod(3), mod(4)] + [full(a) for a in consts],
        out_specs=[tok(d), tok(d)],
        out_shape=[jax.ShapeDtypeStruct((bsz, seq, d), F32), jax.ShapeDtypeStruct((bsz, seq, d), BF16)],
        compiler_params=_params("parallel", "parallel"),
        name="merge",
    )(ya, bg, cx, cx, cx, ga, gb, x, mods3, mods3, mods3, *consts)


FFN_ROW_BLOCK = 256


def _ffn_kernel(h_ref, x1_ref, g2_ref, fg_ref, wa_ref, wv_ref, cwa_ref, cwv_ref, wd_ref, o_ref,
                upa0_ref, upv0_ref, upa1_ref, upv1_ref, act0_ref, act1_ref, x1_buf, x1_sem, *, gw, lag):
    b, j = pl.program_id(0), pl.program_id(1)
    n_b, nj = pl.num_programs(0) - 1, pl.num_programs(1)
    seq, ck = act0_ref.shape
    d = o_ref.shape[2]
    rb = min(FFN_ROW_BLOCK, seq)
    active = jnp.logical_or(b < n_b, j < 2)
    filled = jnp.logical_or(b > 0, j >= 2)
    b2, j2 = lag(b, j, 2)
    odd = (b * nj + j) % 2 == 1
    draining = jnp.logical_and(active, filled)

    def x1_copy():
        return pltpu.make_async_copy(x1_ref.at[b2], x1_buf, x1_sem)

    @pl.when(jnp.logical_and(draining, j2 == 0))
    def _():
        x1_copy().start()

    @pl.when(jnp.logical_and(b == 0, j == 0))
    def _():
        for ref in (upa0_ref, upv0_ref, upa1_ref, upv1_ref, act0_ref, act1_ref):
            ref[...] = jnp.zeros(ref.shape, BF16)

    @pl.when(jnp.logical_and(active, j2 == 0))
    def _():
        o_ref[0] = jnp.zeros(o_ref.shape[1:], F32)

    col = lax.broadcasted_iota(jnp.int32, (gw, ck), 0)
    first_col = col == 0
    last_col = col == gw - 1

    def stages(up_w, up_r, act_w, act_r):
        def up_piece(lo, w_ref, dst):
            def run():
                u = jnp.dot(h_ref[0, lo:lo + rb, :], w_ref[...], preferred_element_type=F32)
                for g0 in range(0, rb, gw):
                    ug = u[g0:g0 + gw, :]
                    r = gw + lo + g0
                    dst[0, r:r + gw, :] = jnp.where(first_col, 0.0, pltpu.roll(ug, 1, 0)).astype(BF16)
                    dst[1, r:r + gw, :] = ug.astype(BF16)
                    dst[2, r:r + gw, :] = jnp.where(last_col, 0.0, pltpu.roll(ug, gw - 1, 0)).astype(BF16)
            return run

        def down_piece(lo):
            def run():
                o_ref[0, lo:lo + rb, :] += jnp.dot(act_r[lo:lo + rb, :], wd_ref[...], preferred_element_type=F32)
            return run

        cw = [cwa_ref[...].astype(BF16), cwv_ref[...].astype(BF16)]

        def conv_piece(base, l0):
            def run():
                acc = []
                for src, w in zip(up_r, cw):
                    t = None
                    for dy in range(3):
                        for dx in range(3):
                            k = dy * 3 + dx
                            term = w[k:k + 1, l0:l0 + LANES] * src[dx, base + dy * gw:base + (dy + 1) * gw,
                                                                   l0:l0 + LANES]
                            t = term if t is None else t + term
                    acc.append(t)
                a, v = acc
                act_w[base:base + gw, l0:l0 + LANES] = a * jax.nn.sigmoid(a) * v
            return run

        mxu = []
        for lo in range(0, seq, rb):
            mxu += [(d, up_piece(lo, wa_ref, up_w[0])), (d, up_piece(lo, wv_ref, up_w[1])), (d, down_piece(lo))]
        vpu = [conv_piece(base, l0) for base in range(0, seq, gw) for l0 in range(0, ck, LANES)]
        total = sum(c for c, _ in mxu)
        done, k = 0, 0
        for c, piece in mxu:
            piece()
            done += c
            while k < len(vpu) and (k + 1) * total <= done * len(vpu):
                vpu[k]()
                k += 1
        for piece in vpu[k:]:
            piece()

    @pl.when(jnp.logical_and(active, jnp.logical_not(odd)))
    def _():
        stages((upa0_ref, upv0_ref), (upa1_ref, upv1_ref), act1_ref, act0_ref)

    @pl.when(jnp.logical_and(active, odd))
    def _():
        stages((upa1_ref, upv1_ref), (upa0_ref, upv0_ref), act0_ref, act1_ref)

    @pl.when(jnp.logical_and(draining, j2 == nj - 1))
    def _():
        x1_copy().wait()
        x2 = x1_buf[...] + g2_ref[0] * o_ref[0]
        o_ref[0] = (x2 * _rms(x2)) * fg_ref[...]


def _ffn(h2, x1, mods3, final_g, w_up, conv_w, w_down, ck):
    bsz, seq, d = x1.shape
    hid = w_down.shape[0]
    nj = hid // ck
    taps = conv_w.shape[0]
    assert nj >= 2

    def lag(b, j, k):
        jj = j - k
        borrow = jj < 0
        bb = jnp.where(borrow, b - 1, b)
        jj = jnp.where(borrow, jj + nj, jj)
        before = bb < 0
        after = bb >= bsz
        bb = jnp.where(before, 0, jnp.where(after, bsz - 1, bb))
        jj = jnp.where(before, 0, jnp.where(after, nj - 1, jj))
        return bb, jj

    return pl.pallas_call(
        functools.partial(_ffn_kernel, gw=GRID_W, lag=lag),
        grid=(bsz + 1, nj),
        in_specs=[pl.BlockSpec((1, seq, d), lambda b, j: (lag(b, j, 0)[0], 0, 0)),
                  pl.BlockSpec(memory_space=pl.ANY),
                  pl.BlockSpec((1, 1, d), lambda b, j: (lag(b, j, 2)[0], 0, 5)),
                  pl.BlockSpec((1, d), lambda b, j: (0, 0)),
                  pl.BlockSpec((d, ck), lambda b, j: (0, lag(b, j, 0)[1])),
                  pl.BlockSpec((d, ck), lambda b, j: (0, nj + lag(b, j, 0)[1])),
                  pl.BlockSpec((taps, ck), lambda b, j: (0, lag(b, j, 1)[1])),
                  pl.BlockSpec((taps, ck), lambda b, j: (0, nj + lag(b, j, 1)[1])),
                  pl.BlockSpec((ck, d), lambda b, j: (lag(b, j, 2)[1], 0))],
        out_specs=pl.BlockSpec((1, seq, d), lambda b, j: (lag(b, j, 2)[0], 0, 0)),
        out_shape=jax.ShapeDtypeStruct((bsz, seq, d), F32),
        scratch_shapes=([pltpu.VMEM((3, seq + 2 * GRID_W, ck), BF16)] * 4 + [pltpu.VMEM((seq, ck), BF16)] * 2
                        + [pltpu.VMEM((seq, d), F32), pltpu.SemaphoreType.DMA(())]),
        compiler_params=_params("arbitrary", "arbitrary"),
        name="ffn",
    )(h2, x1, mods3, final_g, w_up, w_up, conv_w, conv_w, w_down)


def _layer(x, ctx, mods3, ctx_row, norm1_g, norm2_g, w_in, ssm, glu_w, glu_b, sconv_w,
           proj_a, proj_b, w_out, ffn_w_up, ffn_conv_w, ffn_w_down, out_g):
    bsz, seq, d = x.shape
    ctx_len = ctx.shape[1]
    n_g, th = ssm[1].shape[0], ssm[1].shape[1]
    sw = glu_w.shape[0]
    cw = sconv_w.shape[1]
    n_h = sw // n_g
    t_chunk = th // n_h
    n_ctx, n_lat = ctx_len // t_chunk, seq // t_chunk
    tm = min(512, seq)
    w_in_b = w_in.astype(BF16)

    u, bg, cx, ga, gb = _inproj(x, mods3, norm1_g, w_in_b, sw, cw, tm)
    u_ctx = _ctxproj(ctx, mods3, ctx_row, norm1_g, w_in_b[:, :sw], min(tm, ctx_len))
    ya = _from_chunks(_ssm(_to_chunks(u_ctx, n_g), _to_chunks(u, n_g), *ssm))

    x1, h2 = _merge(ya, bg, cx, ga, gb, x, mods3, glu_w.astype(BF16), glu_b.reshape(1, sw), sconv_w,
                    proj_a.astype(BF16), proj_b.astype(BF16), w_out.astype(BF16), norm2_g, tm)
    k2 = ffn_conv_w.shape[0] * ffn_conv_w.shape[1]
    return _ffn(h2, x1, mods3, out_g, ffn_w_up.astype(BF16), ffn_conv_w.reshape(k2, -1),
                ffn_w_down.astype(BF16), 256)


def kernel(x, c, ctx, c_ctx, mod_w, mod_b, norm1_g, norm2_g, w_in, ssm_lambda_re, ssm_lambda_im,
           ssm_log_dt, ssm_b_re, ssm_b_im, ssm_c_re, ssm_c_im, ssm_d, ssm_glu_w, ssm_glu_b,
           sconv_w, proj_a, proj_b, w_out, ffn_w_up, ffn_conv_w, ffn_w_down, final_g):
    depth = mod_w.shape[0]
    assert depth == 1, "context-stream update between layers is not implemented"
    bsz, seq, d = x.shape
    assert seq % GRID_W == 0 and seq % SSM_CHUNK == 0 and ctx.shape[1] % SSM_CHUNK == 0
    i = 0
    rows = -(-(bsz + 1) // 8) * 8
    cond = jnp.zeros((rows, d), F32).at[:bsz].set(c).at[bsz].set(c_ctx)
    mods3 = _adaln(cond, mod_w[i], mod_b[i]).reshape(rows, 1, N_MOD * d)
    ssm = _ssm_weights(ssm_lambda_re[i], ssm_lambda_im[i], ssm_log_dt[i], ssm_b_re[i], ssm_b_im[i],
                       ssm_c_re[i], ssm_c_im[i], ssm_d[i], SSM_CHUNK)
    return _layer(x, ctx, mods3, bsz, norm1_g[i].reshape(1, d), norm2_g[i].reshape(1, d), w_in[i], ssm,
                  ssm_glu_w[i], ssm_glu_b[i], sconv_w[i], proj_a[i], proj_b[i], w_out[i],
                  ffn_w_up[i], ffn_conv_w[i], ffn_w_down[i], final_g.reshape(1, d))
```

```python
import functools

import jax
import jax.numpy as jnp
from jax import lax
from jax.experimental import pallas as pl
from jax.experimental.pallas import tpu as pltpu

F32 = jnp.float32
BF16 = jnp.bfloat16

EPS = 1e-6
GRID_W = 64
N_MOD = 6
SSM_CHUNK = 16
SUBLANES = 8
LANES = 128
BF16_SUBLANES = 16
MXU_TILE = 256
VMEM_LIMIT_BYTES = 56 * 1024 * 1024
HIGHEST = lax.Precision.HIGHEST


def _params(*semantics):
    return pltpu.CompilerParams(dimension_semantics=semantics, vmem_limit_bytes=VMEM_LIMIT_BYTES)


def _rms(x):
    return lax.rsqrt(jnp.mean(x * x, axis=-1, keepdims=True) + EPS)


def _adaln_kernel(c_ref, w_ref, b_ref, o_ref):
    s = c_ref[...]
    s = s * jax.nn.sigmoid(s)
    o_ref[...] = jnp.dot(s, w_ref[...], preferred_element_type=F32, precision=HIGHEST) + b_ref[...]


def _adaln(cond, w, b):
    r, d = cond.shape
    n = w.shape[1]
    tn = d
    return pl.pallas_call(
        _adaln_kernel,
        grid=(n // tn,),
        in_specs=[pl.BlockSpec((r, d), lambda j: (0, 0)),
                  pl.BlockSpec((d, tn), lambda j: (0, j)),
                  pl.BlockSpec((1, tn), lambda j: (0, j))],
        out_specs=pl.BlockSpec((r, tn), lambda j: (0, j)),
        out_shape=jax.ShapeDtypeStruct((r, n), F32),
        compiler_params=_params("parallel"),
        name="adaln",
    )(cond, w, b.reshape(1, n))


def _norm_mod(x_ref, sh_ref, sc_ref, g_ref):
    x = x_ref[0]
    return ((x * _rms(x)) * g_ref[...] * (1.0 + sc_ref[0]) + sh_ref[0]).astype(BF16)


def _inproj_kernel(x_ref, sh_ref, sc_ref, g_ref, w_ref, u_ref, bg_ref, cx_ref, ga_ref, gb_ref,
                   *, sw, cw, d):
    hb = _norm_mod(x_ref, sh_ref, sc_ref, g_ref)

    def proj(lo, n):
        return jnp.dot(hb, w_ref[:, lo:lo + n], preferred_element_type=F32)

    u_ref[0] = proj(0, sw).astype(BF16)
    bg_ref[0] = proj(sw, cw).astype(BF16)
    cx_ref[0] = (proj(sw + cw, cw) * proj(sw + 2 * cw, cw)).astype(BF16)
    o2 = sw + 3 * cw
    ga_ref[0] = jax.nn.sigmoid(proj(o2, d)).astype(BF16)
    gb_ref[0] = jax.nn.sigmoid(proj(o2 + d, d)).astype(BF16)


def _inproj(x, mods3, g, w, sw, cw, tm):
    bsz, seq, d = x.shape
    cols = w.shape[1]
    tok = lambda n: pl.BlockSpec((1, tm, n), lambda b, i: (b, i, 0))
    mod = lambda k: pl.BlockSpec((1, 1, d), lambda b, i: (b, 0, k))
    out = lambda n: jax.ShapeDtypeStruct((bsz, seq, n), BF16)
    return pl.pallas_call(
        functools.partial(_inproj_kernel, sw=sw, cw=cw, d=d),
        grid=(bsz, seq // tm),
        in_specs=[tok(d), mod(0), mod(1),
                  pl.BlockSpec((1, d), lambda b, i: (0, 0)),
                  pl.BlockSpec((d, cols), lambda b, i: (0, 0))],
        out_specs=[tok(sw), tok(cw), tok(cw), tok(d), tok(d)],
        out_shape=[out(sw), out(cw), out(cw), out(d), out(d)],
        compiler_params=_params("parallel", "parallel"),
        name="inproj",
    )(x, mods3, mods3, g, w)


def _ctxproj_kernel(x_ref, sh_ref, sc_ref, g_ref, w_ref, u_ref):
    hb = _norm_mod(x_ref, sh_ref, sc_ref, g_ref)
    u_ref[0] = jnp.dot(hb, w_ref[...], preferred_element_type=F32).astype(BF16)


def _ctxproj(ctx, mods3, row, g, w, tm):
    bsz, seq, d = ctx.shape
    sw = w.shape[1]
    mod = lambda k: pl.BlockSpec((1, 1, d), lambda b, i: (row, 0, k))
    return pl.pallas_call(
        _ctxproj_kernel,
        grid=(bsz, seq // tm),
        in_specs=[pl.BlockSpec((1, tm, d), lambda b, i: (b, i, 0)), mod(0), mod(1),
                  pl.BlockSpec((1, d), lambda b, i: (0, 0)),
                  pl.BlockSpec((d, sw), lambda b, i: (0, 0))],
        out_specs=pl.BlockSpec((1, tm, sw), lambda b, i: (b, i, 0)),
        out_shape=jax.ShapeDtypeStruct((bsz, seq, sw), BF16),
        compiler_params=_params("parallel", "parallel"),
        name="ctxproj",
    )(ctx, mods3, mods3, g, w)


RELAYOUT_BATCH = BF16_SUBLANES
RELAYOUT_CHUNKS = 4
RELAYOUT_TILES_IN_FLIGHT = 4


def _block_transpose(groups, width):
    n = len(groups[0])
    shape = groups[0][0].shape
    lanes = shape[-1]
    block = lax.broadcasted_iota(jnp.int32, shape, len(shape) - 1) // width
    s = n // 2
    while s:
        keep = (block & s) == 0
        nxt = []
        for regs in groups:
            new = list(regs)
            for i in range(n):
                if i & s == 0:
                    a, b = regs[i], regs[i | s]
                    new[i] = jnp.where(keep, a, pltpu.roll(b, s * width, 1))
                    new[i | s] = jnp.where(keep, pltpu.roll(a, lanes - s * width, 1), b)
            nxt.append(new)
        groups = nxt
        s //= 2
    return groups


def _chunk_tiles(ntok, width, n_h, t_chunk):
    per = LANES // n_h
    n_q = width // LANES
    step = min(RELAYOUT_TILES_IN_FLIGHT, n_q)
    for cl in range(ntok // t_chunk):
        for hf in range(t_chunk // per):
            for q0 in range(0, n_q, step):
                yield cl, hf, range(q0, q0 + step), per, cl * t_chunk + hf * per


def _to_chunks_kernel(u_ref, o_ref, s_ref, *, n_h):
    nb, ntok, width = u_ref.shape
    t_chunk = o_ref.shape[3] // n_h
    pitch = s_ref.shape[1] // nb
    for q in range(width // LANES):
        for b in range(nb):
            s_ref[q, b * pitch:b * pitch + ntok, :] = u_ref[b, :, q * LANES:(q + 1) * LANES].astype(F32)
    for cl, hf, qs, per, tok in _chunk_tiles(ntok, width, n_h, t_chunk):
        groups = [[s_ref[q, pl.ds(tok + j, nb, stride=pitch), :] for j in range(per)] for q in qs]
        for q, regs in zip(qs, _block_transpose(groups, n_h)):
            for gl, r in enumerate(regs):
                o_ref[q * per + gl, cl, :, hf * LANES:(hf + 1) * LANES] = r.astype(BF16)


def _from_chunks_kernel(y_ref, o_ref, s_ref, *, n_h):
    nb, ntok, width = o_ref.shape
    t_chunk = y_ref.shape[3] // n_h
    pitch = s_ref.shape[1] // nb
    for cl, hf, qs, per, tok in _chunk_tiles(ntok, width, n_h, t_chunk):
        groups = [[y_ref[q * per + gl, cl, :, hf * LANES:(hf + 1) * LANES].astype(F32) for gl in range(per)]
                  for q in qs]
        for q, regs in zip(qs, _block_transpose(groups, n_h)):
            for j, r in enumerate(regs):
                s_ref[q, pl.ds(tok + j, nb, stride=pitch), :] = r
    for q in range(width // LANES):
        for b in range(nb):
            o_ref[b, :, q * LANES:(q + 1) * LANES] = s_ref[q, b * pitch:b * pitch + ntok, :].astype(BF16)


def _chunk_specs(bsz, seq, width, n_g):
    n_h = width // n_g
    nb = min(RELAYOUT_BATCH, bsz)
    ntok = RELAYOUT_CHUNKS * SSM_CHUNK
    assert bsz % nb == 0 and seq % ntok == 0 and LANES % n_h == 0 and width % LANES == 0
    nat = pl.BlockSpec((nb, ntok, width), lambda bi, ti: (bi, ti, 0))
    chunked = pl.BlockSpec((n_g, RELAYOUT_CHUNKS, nb, SSM_CHUNK * n_h), lambda bi, ti: (0, ti, bi, 0))
    pitch = ntok + SUBLANES if (ntok // SUBLANES) % 2 == 0 else ntok
    scratch = pltpu.VMEM((width // LANES, nb * pitch, LANES), F32)
    return dict(grid=(bsz // nb, seq // ntok), scratch_shapes=[scratch],
                compiler_params=_params("parallel", "parallel")), nat, chunked, n_h


def _to_chunks(u, n_g):
    bsz, seq, width = u.shape
    common, nat, chunked, n_h = _chunk_specs(bsz, seq, width, n_g)
    return pl.pallas_call(
        functools.partial(_to_chunks_kernel, n_h=n_h), in_specs=[nat], out_specs=chunked,
        out_shape=jax.ShapeDtypeStruct((n_g, seq // SSM_CHUNK, bsz, SSM_CHUNK * n_h), BF16),
        name="to_chunks", **common)(u)


def _from_chunks(y):
    n_g, n_chunks, bsz, th = y.shape
    n_h = th // SSM_CHUNK
    seq = n_chunks * SSM_CHUNK
    common, nat, chunked, n_h = _chunk_specs(bsz, seq, n_g * n_h, n_g)
    return pl.pallas_call(
        functools.partial(_from_chunks_kernel, n_h=n_h), in_specs=[chunked], out_specs=nat,
        out_shape=jax.ShapeDtypeStruct((bsz, seq, n_g * n_h), BF16),
        name="from_chunks", **common)(y)


LAG_GROUPS_PER_STEP = 8


def _lag_kernel(e_ref, b_ref, o_ref):
    for dr in range(e_ref.shape[0]):
        for g in range(e_ref.shape[1]):
            o_ref[dr, g] = jnp.dot(e_ref[dr, g], b_ref[dr, g], preferred_element_type=F32, precision=HIGHEST)


def _lag_kernels(lhs, rhs):
    n_dir, n_g, rows, k = lhs.shape
    n_h = rhs.shape[3]
    gs = LAG_GROUPS_PER_STEP if n_g % LAG_GROUPS_PER_STEP == 0 else 1
    spec = lambda r, c: pl.BlockSpec((n_dir, gs, r, c), lambda g: (0, g, 0, 0))
    return pl.pallas_call(
        _lag_kernel,
        grid=(n_g // gs,),
        in_specs=[spec(rows, k), spec(k, n_h)],
        out_specs=spec(rows, n_h),
        out_shape=jax.ShapeDtypeStruct((n_dir, n_g, rows, n_h), F32),
        compiler_params=_params("parallel"),
        name="lag_kernels",
    )(lhs, rhs)


def _ssm_weights(lam_re, lam_im, log_dt, b_re, b_im, c_re, c_im, d_skip, t_chunk):
    n_dir, n_g, n_p = lam_re.shape
    n_h = b_re.shape[-1]
    dt = jnp.exp(log_dt)[..., None]
    xr, xi = lam_re * dt, lam_im * dt
    mag = jnp.exp(xr)
    a_re, a_im = mag * jnp.cos(xi), mag * jnp.sin(xi)
    n_re = jnp.expm1(xr) * jnp.cos(xi) - 2.0 * jnp.sin(0.5 * xi) ** 2
    n_im = a_im
    den = lam_re * lam_re + lam_im * lam_im
    q_re = (n_re * lam_re + n_im * lam_im) / den
    q_im = (n_im * lam_re - n_re * lam_im) / den
    bb_re = q_re[..., None] * b_re - q_im[..., None] * b_im
    bb_im = q_re[..., None] * b_im + q_im[..., None] * b_re

    pw_re, pw_im = [jnp.ones_like(a_re)], [jnp.zeros_like(a_im)]
    for _ in range(t_chunk):
        pr, pi = pw_re[-1], pw_im[-1]
        pw_re.append(pr * a_re - pi * a_im)
        pw_im.append(pr * a_im + pi * a_re)
    pw_re, pw_im = jnp.stack(pw_re), jnp.stack(pw_im)

    def by_token(pw, lo, flip_dir):
        p = pw[lo:lo + t_chunk]
        return jnp.stack([p[::-1, d] if d == flip_dir else p[:, d] for d in range(n_dir)], axis=1)

    pin_re, pin_im = by_token(pw_re, 0, 0), by_token(pw_im, 0, 0)
    win_re = pin_re[..., None] * bb_re[None] - pin_im[..., None] * bb_im[None]
    win_im = pin_re[..., None] * bb_im[None] + pin_im[..., None] * bb_re[None]

    def to_in(w, k):
        return jnp.transpose(w[:, k], (1, 0, 3, 2)).reshape(n_g, t_chunk * n_h, n_p)

    w_in = jnp.concatenate([to_in(win_re, 0), to_in(win_re, 1), to_in(win_im, 0), to_in(win_im, 1)], axis=-1)

    pout_re, pout_im = by_token(pw_re, 1, 1), by_token(pw_im, 1, 1)
    cr, ci = c_re[None], c_im[None]
    e_re = cr * pout_re[:, :, :, None, :] - ci * pout_im[:, :, :, None, :]
    e_im = cr * pout_im[:, :, :, None, :] + ci * pout_re[:, :, :, None, :]

    def to_out(w, k):
        return jnp.transpose(w[:, k], (1, 3, 0, 2)).reshape(n_g, n_p, t_chunk * n_h)

    w_out = jnp.concatenate([to_out(e_re, 0), to_out(e_re, 1), to_out(-e_im, 0), to_out(-e_im, 1)], axis=1)

    ca_re = cr * pw_re[:t_chunk, :, :, None, :] - ci * pw_im[:t_chunk, :, :, None, :]
    ca_im = cr * pw_im[:t_chunk, :, :, None, :] + ci * pw_re[:t_chunk, :, :, None, :]
    lhs = jnp.concatenate([ca_re, -ca_im], axis=-1)
    lhs = jnp.transpose(lhs, (1, 2, 0, 3, 4)).reshape(n_dir, n_g, t_chunk * n_h, 2 * n_p)
    kk = _lag_kernels(lhs, jnp.concatenate([bb_re, bb_im], axis=2))
    kk_t = jnp.transpose(kk.reshape(n_dir, n_g, t_chunk, n_h, n_h), (0, 1, 4, 2, 3))
    k_0 = kk_t[0, :, :, 0] + kk_t[1, :, :, 0] + jnp.eye(n_h, dtype=F32)[None] * d_skip.reshape(n_g, n_h, 1)
    strip = jnp.concatenate([kk_t[1, :, :, :0:-1].reshape(n_g, n_h, -1), k_0,
                             kk_t[0, :, :, 1:].reshape(n_g, n_h, -1)], axis=-1)
    th = t_chunk * n_h
    m = jnp.stack([strip[:, :, (t_chunk - 1 - t) * n_h:(t_chunk - 1 - t) * n_h + th] for t in range(t_chunk)],
                  axis=1).reshape(n_g, th, th)

    at_re = jnp.concatenate([pw_re[t_chunk, 0], pw_re[t_chunk, 1]], axis=-1)[:, None, :]
    at_im = jnp.concatenate([pw_im[t_chunk, 0], pw_im[t_chunk, 1]], axis=-1)[:, None, :]
    return w_in.astype(BF16), m.astype(BF16), w_out.astype(BF16), at_re, at_im


def _ssm_kernel(uc_ref, ul_ref, win_ref, m_ref, wout_ref, are_ref, aim_ref, y_ref, s_ref, xin_ref,
                *, nb, n_ctx, n_lat):
    n_all = n_ctx + n_lat
    lat = n_ctx * nb
    half = s_ref.shape[1] // 2
    quarter = half // 2
    u_ctx = uc_ref[0].reshape(lat, uc_ref.shape[3])
    u_lat = ul_ref[0].reshape(n_lat * nb, ul_ref.shape[3])
    s_ref[0:lat, :] = jnp.dot(u_ctx, win_ref[0], preferred_element_type=F32)
    s_ref[lat:, :] = jnp.dot(u_lat, win_ref[0], preferred_element_type=F32)
    a_re = are_ref[0]
    a_im = aim_ref[0]
    is_f = lax.broadcasted_iota(jnp.int32, (nb, half), 1) < quarter

    def step(k, carry):
        x_re, x_im = carry
        cb = jnp.where(k < n_ctx, n_ctx - 1 - k, n_all + n_ctx - 1 - k)
        rf = pl.multiple_of(k * nb, nb)
        rb = pl.multiple_of(cb * nb, nb)
        xin_ref[pl.ds(rf, nb), 0:quarter] = x_re[:, 0:quarter]
        xin_ref[pl.ds(rb, nb), quarter:half] = x_re[:, quarter:half]
        xin_ref[pl.ds(rf, nb), half:half + quarter] = x_im[:, 0:quarter]
        xin_ref[pl.ds(rb, nb), half + quarter:2 * half] = x_im[:, quarter:half]
        s_f = s_ref[pl.ds(rf, nb), :]
        s_b = s_ref[pl.ds(rb, nb), :]
        s_re = jnp.where(is_f, s_f[:, :half], s_b[:, :half])
        s_im = jnp.where(is_f, s_f[:, half:], s_b[:, half:])
        return (a_re * x_re - a_im * x_im + s_re, a_re * x_im + a_im * x_re + s_im)

    zero = jnp.zeros((nb, half), F32)
    lax.fori_loop(0, n_all, step, (zero, zero))

    y = jnp.dot(u_lat, m_ref[0], preferred_element_type=F32)
    y = y + jnp.dot(xin_ref[lat:, :].astype(BF16), wout_ref[0], preferred_element_type=F32)
    y_ref[0] = y.astype(BF16).reshape(y_ref.shape[1:])


def _ssm(u_ctx, u_lat, w_in, m, w_out, at_re, at_im):
    n_g, n_lat, nb, th = u_lat.shape
    n_ctx = u_ctx.shape[1]
    rows = (n_ctx + n_lat) * nb
    ns = w_in.shape[2]
    per_g = lambda a: pl.BlockSpec((1,) + a.shape[1:], lambda g: (g,) + (0,) * (a.ndim - 1))
    ins = (u_ctx, u_lat, w_in, m, w_out, at_re, at_im)
    return pl.pallas_call(
        functools.partial(_ssm_kernel, nb=nb, n_ctx=n_ctx, n_lat=n_lat),
        grid=(n_g,),
        in_specs=[per_g(a) for a in ins],
        out_specs=per_g(u_lat),
        out_shape=jax.ShapeDtypeStruct(u_lat.shape, BF16),
        scratch_shapes=[pltpu.VMEM((rows, ns), F32), pltpu.VMEM((rows, ns), F32)],
        compiler_params=_params("parallel"),
        name="ssm",
    )(*ins)


def _merge_kernel(ya_ref, bg_ref, cx_ref, cxp_ref, cxn_ref, ga_ref, gb_ref, x_ref,
                  g1_ref, sh2_ref, sc2_ref, gluw_ref, glub_ref, scw_ref, pa_ref, pb_ref, wo_ref, n2_ref,
                  x1_ref, h2_ref):
    i = pl.program_id(1)
    tm = cx_ref.shape[1]
    z = jax.nn.gelu(ya_ref[0].astype(F32), approximate=True)
    gate = jax.nn.sigmoid(jnp.dot(z.astype(BF16), gluw_ref[...], preferred_element_type=F32) + glub_ref[...])
    y_a = (z * gate).astype(BF16)

    cx = cx_ref[0].astype(F32)
    prev = cxp_ref[0].astype(F32)[BF16_SUBLANES - 1:BF16_SUBLANES]
    nxt = cxn_ref[0].astype(F32)[0:1]
    prev = jnp.where(i > 0, prev, 0.0)
    nxt = jnp.where(i < pl.num_programs(1) - 1, nxt, 0.0)
    row = lax.broadcasted_iota(jnp.int32, (tm, 1), 0)
    c_m1 = jnp.where(row == 0, prev, pltpu.roll(cx, 1, 0))
    c_p1 = jnp.where(row == tm - 1, nxt, pltpu.roll(cx, tm - 1, 0))
    w = scw_ref[...]
    y_b = (bg_ref[0].astype(F32) * (w[0:1] * c_m1 + w[1:2] * cx + w[2:3] * c_p1)).astype(BF16)

    merged = (ga_ref[0].astype(F32) * jnp.dot(y_a, pa_ref[...], preferred_element_type=F32)
              + gb_ref[0].astype(F32) * jnp.dot(y_b, pb_ref[...], preferred_element_type=F32))
    x1 = x_ref[0] + g1_ref[0] * jnp.dot(merged.astype(BF16), wo_ref[...], preferred_element_type=F32)
    x1_ref[0] = x1
    h2_ref[0] = ((x1 * _rms(x1)) * n2_ref[...] * (1.0 + sc2_ref[0]) + sh2_ref[0]).astype(BF16)


def _merge(ya, bg, cx, ga, gb, x, mods3, glu_w, glu_b, sconv_w, proj_a, proj_b, w_out, n2, tm):
    bsz, seq, d = x.shape
    sw, cw = ya.shape[2], cx.shape[2]
    hb = BF16_SUBLANES
    last_hb = seq // hb - 1
    tok = lambda n: pl.BlockSpec((1, tm, n), lambda b, i: (b, i, 0))
    mod = lambda k: pl.BlockSpec((1, 1, d), lambda b, i: (b, 0, k))
    full = lambda a: pl.BlockSpec(a.shape, lambda b, i: (0,) * a.ndim)
    prev_spec = pl.BlockSpec((1, hb, cw), lambda b, i: (b, jnp.maximum(i * (tm // hb) - 1, 0), 0))
    next_spec = pl.BlockSpec((1, hb, cw), lambda b, i: (b, jnp.minimum((i + 1) * (tm // hb), last_hb), 0))
    consts = (glu_w, glu_b, sconv_w, proj_a, proj_b, w_out, n2)
    return pl.pallas_call(
        _merge_kernel,
        grid=(bsz, seq // tm),
        in_specs=[tok(sw), tok(cw), tok(cw), prev_spec, next_spec, tok(d), tok(d), tok(d),
                  mod(2), mod(3), mod(4)] + [full(a) for a in consts],
        out_specs=[tok(d), tok(d)],
        out_shape=[jax.ShapeDtypeStruct((bsz, seq, d), F32), jax.ShapeDtypeStruct((bsz, seq, d), BF16)],
        compiler_params=_params("parallel", "parallel"),
        name="merge",
    )(ya, bg, cx, cx, cx, ga, gb, x, mods3, mods3, mods3, *consts)


FFN_ROW_BLOCK = 256


def _ffn_kernel(h_ref, x1_ref, g2_ref, fg_ref, wa_ref, wv_ref, cwa_ref, cwv_ref, wd_ref, o_ref,
                upa0_ref, upv0_ref, upa1_ref, upv1_ref, act0_ref, act1_ref, x1_buf, x1_sem, *, gw, lag):
    b, j = pl.program_id(0), pl.program_id(1)
    n_b, nj = pl.num_programs(0) - 1, pl.num_programs(1)
    seq, ck = act0_ref.shape
    d = o_ref.shape[2]
    rb = min(FFN_ROW_BLOCK, seq)
    active = jnp.logical_or(b < n_b, j < 2)
    filled = jnp.logical_or(b > 0, j >= 2)
    b2, j2 = lag(b, j, 2)
    odd = (b * nj + j) % 2 == 1
    draining = jnp.logical_and(active, filled)

    def x1_copy():
        return pltpu.make_async_copy(x1_ref.at[b2], x1_buf, x1_sem)

    @pl.when(jnp.logical_and(draining, j2 == 0))
    def _():
        x1_copy().start()

    @pl.when(jnp.logical_and(b == 0, j == 0))
    def _():
        for ref in (upa0_ref, upv0_ref, upa1_ref, upv1_ref, act0_ref, act1_ref):
            ref[...] = jnp.zeros(ref.shape, BF16)

    @pl.when(jnp.logical_and(active, j2 == 0))
    def _():
        o_ref[0] = jnp.zeros(o_ref.shape[1:], F32)

    col = lax.broadcasted_iota(jnp.int32, (gw, ck), 0)
    first_col = col == 0
    last_col = col == gw - 1

    def stages(up_w, up_r, act_w, act_r):
        def up_piece(lo, w_ref, dst):
            def run():
                u = jnp.dot(h_ref[0, lo:lo + rb, :], w_ref[...], preferred_element_type=F32)
                for g0 in range(0, rb, gw):
                    ug = u[g0:g0 + gw, :]
                    r = gw + lo + g0
                    dst[0, r:r + gw, :] = jnp.where(first_col, 0.0, pltpu.roll(ug, 1, 0)).astype(BF16)
                    dst[1, r:r + gw, :] = ug.astype(BF16)
                    dst[2, r:r + gw, :] = jnp.where(last_col, 0.0, pltpu.roll(ug, gw - 1, 0)).astype(BF16)
            return run

        def down_piece(lo):
            def run():
                o_ref[0, lo:lo + rb, :] += jnp.dot(act_r[lo:lo + rb, :], wd_ref[...], preferred_element_type=F32)
            return run

        cw = [cwa_ref[...].astype(BF16), cwv_ref[...].astype(BF16)]

        def conv_piece(base, l0):
            def run():
                acc = []
                for src, w in zip(up_r, cw):
                    t = None
                    for dy in range(3):
                        for dx in range(3):
                            k = dy * 3 + dx
                            term = w[k:k + 1, l0:l0 + LANES] * src[dx, base + dy * gw:base + (dy + 1) * gw,
                                                                   l0:l0 + LANES]
                            t = term if t is None else t + term
                    acc.append(t)
                a, v = acc
                act_w[base:base + gw, l0:l0 + LANES] = a * jax.nn.sigmoid(a) * v
            return run

        mxu = []
        for lo in range(0, seq, rb):
            mxu += [(d, up_piece(lo, wa_ref, up_w[0])), (d, up_piece(lo, wv_ref, up_w[1])), (d, down_piece(lo))]
        vpu = [conv_piece(base, l0) for base in range(0, seq, gw) for l0 in range(0, ck, LANES)]
        total = sum(c for c, _ in mxu)
        done, k = 0, 0
        for c, piece in mxu:
            piece()
            done += c
            while k < len(vpu) and (k + 1) * total <= done * len(vpu):
                vpu[k]()
                k += 1
        for piece in vpu[k:]:
            piece()

    @pl.when(jnp.logical_and(active, jnp.logical_not(odd)))
    def _():
        stages((upa0_ref, upv0_ref), (upa1_ref, upv1_ref), act1_ref, act0_ref)

    @pl.when(jnp.logical_and(active, odd))
    def _():
        stages((upa1_ref, upv1_ref), (upa0_ref, upv0_ref), act0_ref, act1_ref)

    @pl.when(jnp.logical_and(draining, j2 == nj - 1))
    def _():
        x1_copy().wait()
        x2 = x1_buf[...] + g2_ref[0] * o_ref[0]
        o_ref[0] = (x2 * _rms(x2)) * fg_ref[...]


def _ffn(h2, x1, mods3, final_g, w_up, conv_w, w_down, ck):
    bsz, seq, d = x1.shape
    hid = w_down.shape[0]
    nj = hid // ck
    taps = conv_w.shape[0]
    assert nj >= 2

    def lag(b, j, k):
        jj = j - k
        borrow = jj < 0
        bb = jnp.where(borrow, b - 1, b)
        jj = jnp.where(borrow, jj + nj, jj)
        before = bb < 0
        after = bb >= bsz
        bb = jnp.where(before, 0, jnp.where(after, bsz - 1, bb))
        jj = jnp.where(before, 0, jnp.where(after, nj - 1, jj))
        return bb, jj

    return pl.pallas_call(
        functools.partial(_ffn_kernel, gw=GRID_W, lag=lag),
        grid=(bsz + 1, nj),
        in_specs=[pl.BlockSpec((1, seq, d), lambda b, j: (lag(b, j, 0)[0], 0, 0)),
                  pl.BlockSpec(memory_space=pl.ANY),
                  pl.BlockSpec((1, 1, d), lambda b, j: (lag(b, j, 2)[0], 0, 5)),
                  pl.BlockSpec((1, d), lambda b, j: (0, 0)),
                  pl.BlockSpec((d, ck), lambda b, j: (0, lag(b, j, 0)[1])),
                  pl.BlockSpec((d, ck), lambda b, j: (0, nj + lag(b, j, 0)[1])),
                  pl.BlockSpec((taps, ck), lambda b, j: (0, lag(b, j, 1)[1])),
                  pl.BlockSpec((taps, ck), lambda b, j: (0, nj + lag(b, j, 1)[1])),
                  pl.BlockSpec((ck, d), lambda b, j: (lag(b, j, 2)[1], 0))],
        out_specs=pl.BlockSpec((1, seq, d), lambda b, j: (lag(b, j, 2)[0], 0, 0)),
        out_shape=jax.ShapeDtypeStruct((bsz, seq, d), F32),
        scratch_shapes=([pltpu.VMEM((3, seq + 2 * GRID_W, ck), BF16)] * 4 + [pltpu.VMEM((seq, ck), BF16)] * 2
                        + [pltpu.VMEM((seq, d), F32), pltpu.SemaphoreType.DMA(())]),
        compiler_params=_params("arbitrary", "arbitrary"),
        name="ffn",
    )(h2, x1, mods3, final_g, w_up, w_up, conv_w, conv_w, w_down)


def _layer(x, ctx, mods3, ctx_row, norm1_g, norm2_g, w_in, ssm, glu_w, glu_b, sconv_w,
           proj_a, proj_b, w_out, ffn_w_up, ffn_conv_w, ffn_w_down, out_g):
    bsz, seq, d = x.shape
    ctx_len = ctx.shape[1]
    n_g, th = ssm[1].shape[0], ssm[1].shape[1]
    sw = glu_w.shape[0]
    cw = sconv_w.shape[1]
    n_h = sw // n_g
    t_chunk = th // n_h
    n_ctx, n_lat = ctx_len // t_chunk, seq // t_chunk
    tm = min(1024, seq)
    w_in_b = w_in.astype(BF16)

    u, bg, cx, ga, gb = _inproj(x, mods3, norm1_g, w_in_b, sw, cw, tm)
    u_ctx = _ctxproj(ctx, mods3, ctx_row, norm1_g, w_in_b[:, :sw], min(tm, ctx_len))
    ya = _from_chunks(_ssm(_to_chunks(u_ctx, n_g), _to_chunks(u, n_g), *ssm))

    x1, h2 = _merge(ya, bg, cx, ga, gb, x, mods3, glu_w.astype(BF16), glu_b.reshape(1, sw), sconv_w,
                    proj_a.astype(BF16), proj_b.astype(BF16), w_out.astype(BF16), norm2_g, tm)
    k2 = ffn_conv_w.shape[0] * ffn_conv_w.shape[1]
    return _ffn(h2, x1, mods3, out_g, ffn_w_up.astype(BF16), ffn_conv_w.reshape(k2, -1),
                ffn_w_down.astype(BF16), 256)


def kernel(x, c, ctx, c_ctx, mod_w, mod_b, norm1_g, norm2_g, w_in, ssm_lambda_re, ssm_lambda_im,
           ssm_log_dt, ssm_b_re, ssm_b_im, ssm_c_re, ssm_c_im, ssm_d, ssm_glu_w, ssm_glu_b,
           sconv_w, proj_a, proj_b, w_out, ffn_w_up, ffn_conv_w, ffn_w_down, final_g):
    depth = mod_w.shape[0]
    assert depth == 1, "context-stream update between layers is not implemented"
    bsz, seq, d = x.shape
    assert seq % GRID_W == 0 and seq % SSM_CHUNK == 0 and ctx.shape[1] % SSM_CHUNK == 0
    i = 0
    rows = -(-(bsz + 1) // 8) * 8
    cond = jnp.zeros((rows, d), F32).at[:bsz].set(c).at[bsz].set(c_ctx)
    mods3 = _adaln(cond, mod_w[i], mod_b[i]).reshape(rows, 1, N_MOD * d)
    ssm = _ssm_weights(ssm_lambda_re[i], ssm_lambda_im[i], ssm_log_dt[i], ssm_b_re[i], ssm_b_im[i],
                       ssm_c_re[i], ssm_c_im[i], ssm_d[i], SSM_CHUNK)
    return _layer(x, ctx, mods3, bsz, norm1_g[i].reshape(1, d), norm2_g[i].reshape(1, d), w_in[i], ssm,
                  ssm_glu_w[i], ssm_glu_b[i], sconv_w[i], proj_a[i], proj_b[i], w_out[i],
                  ffn_w_up[i], ffn_conv_w[i], ffn_w_down[i], final_g.reshape(1, d))
```

```python
import functools

import jax
import jax.numpy as jnp
from jax import lax
from jax.experimental import pallas as pl
from jax.experimental.pallas import tpu as pltpu

F32 = jnp.float32
BF16 = jnp.bfloat16

EPS = 1e-6
GRID_W = 64
N_MOD = 6
SSM_CHUNK = 32
SUBLANES = 8
LANES = 128
BF16_SUBLANES = 16
MXU_TILE = 256
VMEM_LIMIT_BYTES = 56 * 1024 * 1024
HIGHEST = lax.Precision.HIGHEST


def _params(*semantics):
    return pltpu.CompilerParams(dimension_semantics=semantics, vmem_limit_bytes=VMEM_LIMIT_BYTES)


def _rms(x):
    return lax.rsqrt(jnp.mean(x * x, axis=-1, keepdims=True) + EPS)


def _adaln_kernel(c_ref, w_ref, b_ref, o_ref):
    s = c_ref[...]
    s = s * jax.nn.sigmoid(s)
    o_ref[...] = jnp.dot(s, w_ref[...], preferred_element_type=F32, precision=HIGHEST) + b_ref[...]


def _adaln(cond, w, b):
    r, d = cond.shape
    n = w.shape[1]
    tn = d
    return pl.pallas_call(
        _adaln_kernel,
        grid=(n // tn,),
        in_specs=[pl.BlockSpec((r, d), lambda j: (0, 0)),
                  pl.BlockSpec((d, tn), lambda j: (0, j)),
                  pl.BlockSpec((1, tn), lambda j: (0, j))],
        out_specs=pl.BlockSpec((r, tn), lambda j: (0, j)),
        out_shape=jax.ShapeDtypeStruct((r, n), F32),
        compiler_params=_params("parallel"),
        name="adaln",
    )(cond, w, b.reshape(1, n))


def _norm_mod(x_ref, sh_ref, sc_ref, g_ref):
    x = x_ref[0]
    return ((x * _rms(x)) * g_ref[...] * (1.0 + sc_ref[0]) + sh_ref[0]).astype(BF16)


def _inproj_kernel(x_ref, sh_ref, sc_ref, g_ref, w_ref, u_ref, bg_ref, cx_ref, ga_ref, gb_ref,
                   *, sw, cw, d):
    hb = _norm_mod(x_ref, sh_ref, sc_ref, g_ref)

    def proj(lo, n):
        return jnp.dot(hb, w_ref[:, lo:lo + n], preferred_element_type=F32)

    u_ref[0] = proj(0, sw).astype(BF16)
    bg_ref[0] = proj(sw, cw).astype(BF16)
    cx_ref[0] = (proj(sw + cw, cw) * proj(sw + 2 * cw, cw)).astype(BF16)
    o2 = sw + 3 * cw
    ga_ref[0] = jax.nn.sigmoid(proj(o2, d)).astype(BF16)
    gb_ref[0] = jax.nn.sigmoid(proj(o2 + d, d)).astype(BF16)


def _inproj(x, mods3, g, w, sw, cw, tm):
    bsz, seq, d = x.shape
    cols = w.shape[1]
    tok = lambda n: pl.BlockSpec((1, tm, n), lambda b, i: (b, i, 0))
    mod = lambda k: pl.BlockSpec((1, 1, d), lambda b, i: (b, 0, k))
    out = lambda n: jax.ShapeDtypeStruct((bsz, seq, n), BF16)
    return pl.pallas_call(
        functools.partial(_inproj_kernel, sw=sw, cw=cw, d=d),
        grid=(bsz, seq // tm),
        in_specs=[tok(d), mod(0), mod(1),
                  pl.BlockSpec((1, d), lambda b, i: (0, 0)),
                  pl.BlockSpec((d, cols), lambda b, i: (0, 0))],
        out_specs=[tok(sw), tok(cw), tok(cw), tok(d), tok(d)],
        out_shape=[out(sw), out(cw), out(cw), out(d), out(d)],
        compiler_params=_params("parallel", "parallel"),
        name="inproj",
    )(x, mods3, mods3, g, w)


def _ctxproj_kernel(x_ref, sh_ref, sc_ref, g_ref, w_ref, u_ref):
    hb = _norm_mod(x_ref, sh_ref, sc_ref, g_ref)
    u_ref[0] = jnp.dot(hb, w_ref[...], preferred_element_type=F32).astype(BF16)


def _ctxproj(ctx, mods3, row, g, w, tm):
    bsz, seq, d = ctx.shape
    sw = w.shape[1]
    mod = lambda k: pl.BlockSpec((1, 1, d), lambda b, i: (row, 0, k))
    return pl.pallas_call(
        _ctxproj_kernel,
        grid=(bsz, seq // tm),
        in_specs=[pl.BlockSpec((1, tm, d), lambda b, i: (b, i, 0)), mod(0), mod(1),
                  pl.BlockSpec((1, d), lambda b, i: (0, 0)),
                  pl.BlockSpec((d, sw), lambda b, i: (0, 0))],
        out_specs=pl.BlockSpec((1, tm, sw), lambda b, i: (b, i, 0)),
        out_shape=jax.ShapeDtypeStruct((bsz, seq, sw), BF16),
        compiler_params=_params("parallel", "parallel"),
        name="ctxproj",
    )(ctx, mods3, mods3, g, w)


RELAYOUT_BATCH = BF16_SUBLANES
RELAYOUT_CHUNKS = 4
RELAYOUT_TILES_IN_FLIGHT = 4


def _block_transpose(groups, width):
    n = len(groups[0])
    shape = groups[0][0].shape
    lanes = shape[-1]
    block = lax.broadcasted_iota(jnp.int32, shape, len(shape) - 1) // width
    s = n // 2
    while s:
        keep = (block & s) == 0
        nxt = []
        for regs in groups:
            new = list(regs)
            for i in range(n):
                if i & s == 0:
                    a, b = regs[i], regs[i | s]
                    new[i] = jnp.where(keep, a, pltpu.roll(b, s * width, 1))
                    new[i | s] = jnp.where(keep, pltpu.roll(a, lanes - s * width, 1), b)
            nxt.append(new)
        groups = nxt
        s //= 2
    return groups


def _chunk_tiles(ntok, width, n_h, t_chunk):
    per = LANES // n_h
    n_q = width // LANES
    step = min(RELAYOUT_TILES_IN_FLIGHT, n_q)
    for cl in range(ntok // t_chunk):
        for hf in range(t_chunk // per):
            for q0 in range(0, n_q, step):
                yield cl, hf, range(q0, q0 + step), per, cl * t_chunk + hf * per


def _to_chunks_kernel(u_ref, o_ref, s_ref, *, n_h):
    nb, ntok, width = u_ref.shape
    t_chunk = o_ref.shape[3] // n_h
    pitch = s_ref.shape[1] // nb
    for q in range(width // LANES):
        for b in range(nb):
            s_ref[q, b * pitch:b * pitch + ntok, :] = u_ref[b, :, q * LANES:(q + 1) * LANES].astype(F32)
    for cl, hf, qs, per, tok in _chunk_tiles(ntok, width, n_h, t_chunk):
        groups = [[s_ref[q, pl.ds(tok + j, nb, stride=pitch), :] for j in range(per)] for q in qs]
        for q, regs in zip(qs, _block_transpose(groups, n_h)):
            for gl, r in enumerate(regs):
                o_ref[q * per + gl, cl, :, hf * LANES:(hf + 1) * LANES] = r.astype(BF16)


def _from_chunks_kernel(y_ref, o_ref, s_ref, *, n_h):
    nb, ntok, width = o_ref.shape
    t_chunk = y_ref.shape[3] // n_h
    pitch = s_ref.shape[1] // nb
    for cl, hf, qs, per, tok in _chunk_tiles(ntok, width, n_h, t_chunk):
        groups = [[y_ref[q * per + gl, cl, :, hf * LANES:(hf + 1) * LANES].astype(F32) for gl in range(per)]
                  for q in qs]
        for q, regs in zip(qs, _block_transpose(groups, n_h)):
            for j, r in enumerate(regs):
                s_ref[q, pl.ds(tok + j, nb, stride=pitch), :] = r
    for q in range(width // LANES):
        for b in range(nb):
            o_ref[b, :, q * LANES:(q + 1) * LANES] = s_ref[q, b * pitch:b * pitch + ntok, :].astype(BF16)


def _chunk_specs(bsz, seq, width, n_g):
    n_h = width // n_g
    nb = min(RELAYOUT_BATCH, bsz)
    ntok = RELAYOUT_CHUNKS * SSM_CHUNK
    assert bsz % nb == 0 and seq % ntok == 0 and LANES % n_h == 0 and width % LANES == 0
    nat = pl.BlockSpec((nb, ntok, width), lambda bi, ti: (bi, ti, 0))
    chunked = pl.BlockSpec((n_g, RELAYOUT_CHUNKS, nb, SSM_CHUNK * n_h), lambda bi, ti: (0, ti, bi, 0))
    pitch = ntok + SUBLANES if (ntok // SUBLANES) % 2 == 0 else ntok
    scratch = pltpu.VMEM((width // LANES, nb * pitch, LANES), F32)
    return dict(grid=(bsz // nb, seq // ntok), scratch_shapes=[scratch],
                compiler_params=_params("parallel", "parallel")), nat, chunked, n_h


def _to_chunks(u, n_g):
    bsz, seq, width = u.shape
    common, nat, chunked, n_h = _chunk_specs(bsz, seq, width, n_g)
    return pl.pallas_call(
        functools.partial(_to_chunks_kernel, n_h=n_h), in_specs=[nat], out_specs=chunked,
        out_shape=jax.ShapeDtypeStruct((n_g, seq // SSM_CHUNK, bsz, SSM_CHUNK * n_h), BF16),
        name="to_chunks", **common)(u)


def _from_chunks(y):
    n_g, n_chunks, bsz, th = y.shape
    n_h = th // SSM_CHUNK
    seq = n_chunks * SSM_CHUNK
    common, nat, chunked, n_h = _chunk_specs(bsz, seq, n_g * n_h, n_g)
    return pl.pallas_call(
        functools.partial(_from_chunks_kernel, n_h=n_h), in_specs=[chunked], out_specs=nat,
        out_shape=jax.ShapeDtypeStruct((bsz, seq, n_g * n_h), BF16),
        name="from_chunks", **common)(y)


LAG_GROUPS_PER_STEP = 8


def _lag_kernel(e_ref, b_ref, o_ref):
    for dr in range(e_ref.shape[0]):
        for g in range(e_ref.shape[1]):
            o_ref[dr, g] = jnp.dot(e_ref[dr, g], b_ref[dr, g], preferred_element_type=F32, precision=HIGHEST)


def _lag_kernels(lhs, rhs):
    n_dir, n_g, rows, k = lhs.shape
    n_h = rhs.shape[3]
    gs = LAG_GROUPS_PER_STEP if n_g % LAG_GROUPS_PER_STEP == 0 else 1
    spec = lambda r, c: pl.BlockSpec((n_dir, gs, r, c), lambda g: (0, g, 0, 0))
    return pl.pallas_call(
        _lag_kernel,
        grid=(n_g // gs,),
        in_specs=[spec(rows, k), spec(k, n_h)],
        out_specs=spec(rows, n_h),
        out_shape=jax.ShapeDtypeStruct((n_dir, n_g, rows, n_h), F32),
        compiler_params=_params("parallel"),
        name="lag_kernels",
    )(lhs, rhs)


def _ssm_weights(lam_re, lam_im, log_dt, b_re, b_im, c_re, c_im, d_skip, t_chunk):
    n_dir, n_g, n_p = lam_re.shape
    n_h = b_re.shape[-1]
    dt = jnp.exp(log_dt)[..., None]
    xr, xi = lam_re * dt, lam_im * dt
    mag = jnp.exp(xr)
    a_re, a_im = mag * jnp.cos(xi), mag * jnp.sin(xi)
    n_re = jnp.expm1(xr) * jnp.cos(xi) - 2.0 * jnp.sin(0.5 * xi) ** 2
    n_im = a_im
    den = lam_re * lam_re + lam_im * lam_im
    q_re = (n_re * lam_re + n_im * lam_im) / den
    q_im = (n_im * lam_re - n_re * lam_im) / den
    bb_re = q_re[..., None] * b_re - q_im[..., None] * b_im
    bb_im = q_re[..., None] * b_im + q_im[..., None] * b_re

    pw_re, pw_im = [jnp.ones_like(a_re)], [jnp.zeros_like(a_im)]
    for _ in range(t_chunk):
        pr, pi = pw_re[-1], pw_im[-1]
        pw_re.append(pr * a_re - pi * a_im)
        pw_im.append(pr * a_im + pi * a_re)
    pw_re, pw_im = jnp.stack(pw_re), jnp.stack(pw_im)

    def by_token(pw, lo, flip_dir):
        p = pw[lo:lo + t_chunk]
        return jnp.stack([p[::-1, d] if d == flip_dir else p[:, d] for d in range(n_dir)], axis=1)

    pin_re, pin_im = by_token(pw_re, 0, 0), by_token(pw_im, 0, 0)
    win_re = pin_re[..., None] * bb_re[None] - pin_im[..., None] * bb_im[None]
    win_im = pin_re[..., None] * bb_im[None] + pin_im[..., None] * bb_re[None]

    def to_in(w, k):
        return jnp.transpose(w[:, k], (1, 0, 3, 2)).reshape(n_g, t_chunk * n_h, n_p)

    w_in = jnp.concatenate([to_in(win_re, 0), to_in(win_re, 1), to_in(win_im, 0), to_in(win_im, 1)], axis=-1)

    pout_re, pout_im = by_token(pw_re, 1, 1), by_token(pw_im, 1, 1)
    cr, ci = c_re[None], c_im[None]
    e_re = cr * pout_re[:, :, :, None, :] - ci * pout_im[:, :, :, None, :]
    e_im = cr * pout_im[:, :, :, None, :] + ci * pout_re[:, :, :, None, :]

    def to_out(w, k):
        return jnp.transpose(w[:, k], (1, 3, 0, 2)).reshape(n_g, n_p, t_chunk * n_h)

    w_out = jnp.concatenate([to_out(e_re, 0), to_out(e_re, 1), to_out(-e_im, 0), to_out(-e_im, 1)], axis=1)

    ca_re = cr * pw_re[:t_chunk, :, :, None, :] - ci * pw_im[:t_chunk, :, :, None, :]
    ca_im = cr * pw_im[:t_chunk, :, :, None, :] + ci * pw_re[:t_chunk, :, :, None, :]
    lhs = jnp.concatenate([ca_re, -ca_im], axis=-1)
    lhs = jnp.transpose(lhs, (1, 2, 0, 3, 4)).reshape(n_dir, n_g, t_chunk * n_h, 2 * n_p)
    kk = _lag_kernels(lhs, jnp.concatenate([bb_re, bb_im], axis=2))
    kk_t = jnp.transpose(kk.reshape(n_dir, n_g, t_chunk, n_h, n_h), (0, 1, 4, 2, 3))
    k_0 = kk_t[0, :, :, 0] + kk_t[1, :, :, 0] + jnp.eye(n_h, dtype=F32)[None] * d_skip.reshape(n_g, n_h, 1)
    strip = jnp.concatenate([kk_t[1, :, :, :0:-1].reshape(n_g, n_h, -1), k_0,
                             kk_t[0, :, :, 1:].reshape(n_g, n_h, -1)], axis=-1)
    th = t_chunk * n_h
    m = jnp.stack([strip[:, :, (t_chunk - 1 - t) * n_h:(t_chunk - 1 - t) * n_h + th] for t in range(t_chunk)],
                  axis=1).reshape(n_g, th, th)

    at_re = jnp.concatenate([pw_re[t_chunk, 0], pw_re[t_chunk, 1]], axis=-1)[:, None, :]
    at_im = jnp.concatenate([pw_im[t_chunk, 0], pw_im[t_chunk, 1]], axis=-1)[:, None, :]
    return w_in.astype(BF16), m.astype(BF16), w_out.astype(BF16), at_re, at_im


def _ssm_kernel(uc_ref, ul_ref, win_ref, m_ref, wout_ref, are_ref, aim_ref, y_ref, s_ref, xin_ref,
                *, nb, n_ctx, n_lat):
    n_all = n_ctx + n_lat
    lat = n_ctx * nb
    half = s_ref.shape[1] // 2
    quarter = half // 2
    u_ctx = uc_ref[0].reshape(lat, uc_ref.shape[3])
    u_lat = ul_ref[0].reshape(n_lat * nb, ul_ref.shape[3])
    s_ref[0:lat, :] = jnp.dot(u_ctx, win_ref[0], preferred_element_type=F32)
    s_ref[lat:, :] = jnp.dot(u_lat, win_ref[0], preferred_element_type=F32)
    a_re = are_ref[0]
    a_im = aim_ref[0]
    is_f = lax.broadcasted_iota(jnp.int32, (nb, half), 1) < quarter

    def step(k, carry):
        x_re, x_im = carry
        cb = jnp.where(k < n_ctx, n_ctx - 1 - k, n_all + n_ctx - 1 - k)
        rf = pl.multiple_of(k * nb, nb)
        rb = pl.multiple_of(cb * nb, nb)
        xin_ref[pl.ds(rf, nb), 0:quarter] = x_re[:, 0:quarter]
        xin_ref[pl.ds(rb, nb), quarter:half] = x_re[:, quarter:half]
        xin_ref[pl.ds(rf, nb), half:half + quarter] = x_im[:, 0:quarter]
        xin_ref[pl.ds(rb, nb), half + quarter:2 * half] = x_im[:, quarter:half]
        s_f = s_ref[pl.ds(rf, nb), :]
        s_b = s_ref[pl.ds(rb, nb), :]
        s_re = jnp.where(is_f, s_f[:, :half], s_b[:, :half])
        s_im = jnp.where(is_f, s_f[:, half:], s_b[:, half:])
        return (a_re * x_re - a_im * x_im + s_re, a_re * x_im + a_im * x_re + s_im)

    zero = jnp.zeros((nb, half), F32)
    lax.fori_loop(0, n_all, step, (zero, zero))

    y = jnp.dot(u_lat, m_ref[0], preferred_element_type=F32)
    y = y + jnp.dot(xin_ref[lat:, :].astype(BF16), wout_ref[0], preferred_element_type=F32)
    y_ref[0] = y.astype(BF16).reshape(y_ref.shape[1:])


def _ssm(u_ctx, u_lat, w_in, m, w_out, at_re, at_im):
    n_g, n_lat, nb, th = u_lat.shape
    n_ctx = u_ctx.shape[1]
    rows = (n_ctx + n_lat) * nb
    ns = w_in.shape[2]
    per_g = lambda a: pl.BlockSpec((1,) + a.shape[1:], lambda g: (g,) + (0,) * (a.ndim - 1))
    ins = (u_ctx, u_lat, w_in, m, w_out, at_re, at_im)
    return pl.pallas_call(
        functools.partial(_ssm_kernel, nb=nb, n_ctx=n_ctx, n_lat=n_lat),
        grid=(n_g,),
        in_specs=[per_g(a) for a in ins],
        out_specs=per_g(u_lat),
        out_shape=jax.ShapeDtypeStruct(u_lat.shape, BF16),
        scratch_shapes=[pltpu.VMEM((rows, ns), F32), pltpu.VMEM((rows, ns), F32)],
        compiler_params=_params("parallel"),
        name="ssm",
    )(*ins)


def _merge_kernel(ya_ref, bg_ref, cx_ref, cxp_ref, cxn_ref, ga_ref, gb_ref, x_ref,
                  g1_ref, sh2_ref, sc2_ref, gluw_ref, glub_ref, scw_ref, pa_ref, pb_ref, wo_ref, n2_ref,
                  x1_ref, h2_ref):
    i = pl.program_id(1)
    tm = cx_ref.shape[1]
    z = jax.nn.gelu(ya_ref[0].astype(F32), approximate=True)
    gate = jax.nn.sigmoid(jnp.dot(z.astype(BF16), gluw_ref[...], preferred_element_type=F32) + glub_ref[...])
    y_a = (z * gate).astype(BF16)

    cx = cx_ref[0].astype(F32)
    prev = cxp_ref[0].astype(F32)[BF16_SUBLANES - 1:BF16_SUBLANES]
    nxt = cxn_ref[0].astype(F32)[0:1]
    prev = jnp.where(i > 0, prev, 0.0)
    nxt = jnp.where(i < pl.num_programs(1) - 1, nxt, 0.0)
    row = lax.broadcasted_iota(jnp.int32, (tm, 1), 0)
    c_m1 = jnp.where(row == 0, prev, pltpu.roll(cx, 1, 0))
    c_p1 = jnp.where(row == tm - 1, nxt, pltpu.roll(cx, tm - 1, 0))
    w = scw_ref[...]
    y_b = (bg_ref[0].astype(F32) * (w[0:1] * c_m1 + w[1:2] * cx + w[2:3] * c_p1)).astype(BF16)

    merged = (ga_ref[0].astype(F32) * jnp.dot(y_a, pa_ref[...], preferred_element_type=F32)
              + gb_ref[0].astype(F32) * jnp.dot(y_b, pb_ref[...], preferred_element_type=F32))
    x1 = x_ref[0] + g1_ref[0] * jnp.dot(merged.astype(BF16), wo_ref[...], preferred_element_type=F32)
    x1_ref[0] = x1
    h2_ref[0] = ((x1 * _rms(x1)) * n2_ref[...] * (1.0 + sc2_ref[0]) + sh2_ref[0]).astype(BF16)


def _merge(ya, bg, cx, ga, gb, x, mods3, glu_w, glu_b, sconv_w, proj_a, proj_b, w_out, n2, tm):
    bsz, seq, d = x.shape
    sw, cw = ya.shape[2], cx.shape[2]
    hb = BF16_SUBLANES
    last_hb = seq // hb - 1
    tok = lambda n: pl.BlockSpec((1, tm, n), lambda b, i: (b, i, 0))
    mod = lambda k: pl.BlockSpec((1, 1, d), lambda b, i: (b, 0, k))
    full = lambda a: pl.BlockSpec(a.shape, lambda b, i: (0,) * a.ndim)
    prev_spec = pl.BlockSpec((1, hb, cw), lambda b, i: (b, jnp.maximum(i * (tm // hb) - 1, 0), 0))
    next_spec = pl.BlockSpec((1, hb, cw), lambda b, i: (b, jnp.minimum((i + 1) * (tm // hb), last_hb), 0))
    consts = (glu_w, glu_b, sconv_w, proj_a, proj_b, w_out, n2)
    return pl.pallas_call(
        _merge_kernel,
        grid=(bsz, seq // tm),
        in_specs=[tok(sw), tok(cw), tok(cw), prev_spec, next_spec, tok(d), tok(d), tok(d),
                  mod(2), mod(3), mod(4)] + [full(a) for a in consts],
        out_specs=[tok(d), tok(d)],
        out_shape=[jax.ShapeDtypeStruct((bsz, seq, d), F32), jax.ShapeDtypeStruct((bsz, seq, d), BF16)],
        compiler_params=_params("parallel", "parallel"),
        name="merge",
    )(ya, bg, cx, cx, cx, ga, gb, x, mods3, mods3, mods3, *consts)


FFN_ROW_BLOCK = 256


def _ffn_kernel(h_ref, x1_ref, g2_ref, fg_ref, wa_ref, wv_ref, cwa_ref, cwv_ref, wd_ref, o_ref,
                upa0_ref, upv0_ref, upa1_ref, upv1_ref, act0_ref, act1_ref, x1_buf, x1_sem, *, gw, lag):
    b, j = pl.program_id(0), pl.program_id(1)
    n_b, nj = pl.num_programs(0) - 1, pl.num_programs(1)
    seq, ck = act0_ref.shape
    d = o_ref.shape[2]
    rb = min(FFN_ROW_BLOCK, seq)
    active = jnp.logical_or(b < n_b, j < 2)
    filled = jnp.logical_or(b > 0, j >= 2)
    b2, j2 = lag(b, j, 2)
    odd = (b * nj + j) % 2 == 1
    draining = jnp.logical_and(active, filled)

    def x1_copy():
        return pltpu.make_async_copy(x1_ref.at[b2], x1_buf, x1_sem)

    @pl.when(jnp.logical_and(draining, j2 == 0))
    def _():
        x1_copy().start()

    @pl.when(jnp.logical_and(b == 0, j == 0))
    def _():
        for ref in (upa0_ref, upv0_ref, upa1_ref, upv1_ref, act0_ref, act1_ref):
            ref[...] = jnp.zeros(ref.shape, BF16)

    @pl.when(jnp.logical_and(active, j2 == 0))
    def _():
        o_ref[0] = jnp.zeros(o_ref.shape[1:], F32)

    col = lax.broadcasted_iota(jnp.int32, (gw, ck), 0)
    first_col = col == 0
    last_col = col == gw - 1

    def stages(up_w, up_r, act_w, act_r):
        def up_piece(lo, w_ref, dst):
            def run():
                u = jnp.dot(h_ref[0, lo:lo + rb, :], w_ref[...], preferred_element_type=F32)
                for g0 in range(0, rb, gw):
                    ug = u[g0:g0 + gw, :]
                    r = gw + lo + g0
                    dst[0, r:r + gw, :] = jnp.where(first_col, 0.0, pltpu.roll(ug, 1, 0)).astype(BF16)
                    dst[1, r:r + gw, :] = ug.astype(BF16)
                    dst[2, r:r + gw, :] = jnp.where(last_col, 0.0, pltpu.roll(ug, gw - 1, 0)).astype(BF16)
            return run

        def down_piece(lo):
            def run():
                o_ref[0, lo:lo + rb, :] += jnp.dot(act_r[lo:lo + rb, :], wd_ref[...], preferred_element_type=F32)
            return run

        cw = [cwa_ref[...].astype(BF16), cwv_ref[...].astype(BF16)]

        def conv_piece(base, l0):
            def run():
                acc = []
                for src, w in zip(up_r, cw):
                    t = None
                    for dy in range(3):
                        for dx in range(3):
                            k = dy * 3 + dx
                            term = w[k:k + 1, l0:l0 + LANES] * src[dx, base + dy * gw:base + (dy + 1) * gw,
                                                                   l0:l0 + LANES]
                            t = term if t is None else t + term
                    acc.append(t)
                a, v = acc
                act_w[base:base + gw, l0:l0 + LANES] = a * jax.nn.sigmoid(a) * v
            return run

        mxu = []
        for lo in range(0, seq, rb):
            mxu += [(d, up_piece(lo, wa_ref, up_w[0])), (d, up_piece(lo, wv_ref, up_w[1])), (d, down_piece(lo))]
        vpu = [conv_piece(base, l0) for base in range(0, seq, gw) for l0 in range(0, ck, LANES)]
        total = sum(c for c, _ in mxu)
        done, k = 0, 0
        for c, piece in mxu:
            piece()
            done += c
            while k < len(vpu) and (k + 1) * total <= done * len(vpu):
                vpu[k]()
                k += 1
        for piece in vpu[k:]:
            piece()

    @pl.when(jnp.logical_and(active, jnp.logical_not(odd)))
    def _():
        stages((upa0_ref, upv0_ref), (upa1_ref, upv1_ref), act1_ref, act0_ref)

    @pl.when(jnp.logical_and(active, odd))
    def _():
        stages((upa1_ref, upv1_ref), (upa0_ref, upv0_ref), act0_ref, act1_ref)

    @pl.when(jnp.logical_and(draining, j2 == nj - 1))
    def _():
        x1_copy().wait()
        x2 = x1_buf[...] + g2_ref[0] * o_ref[0]
        o_ref[0] = (x2 * _rms(x2)) * fg_ref[...]


def _ffn(h2, x1, mods3, final_g, w_up, conv_w, w_down, ck):
    bsz, seq, d = x1.shape
    hid = w_down.shape[0]
    nj = hid // ck
    taps = conv_w.shape[0]
    assert nj >= 2

    def lag(b, j, k):
        jj = j - k
        borrow = jj < 0
        bb = jnp.where(borrow, b - 1, b)
        jj = jnp.where(borrow, jj + nj, jj)
        before = bb < 0
        after = bb >= bsz
        bb = jnp.where(before, 0, jnp.where(after, bsz - 1, bb))
        jj = jnp.where(before, 0, jnp.where(after, nj - 1, jj))
        return bb, jj

    return pl.pallas_call(
        functools.partial(_ffn_kernel, gw=GRID_W, lag=lag),
        grid=(bsz + 1, nj),
        in_specs=[pl.BlockSpec((1, seq, d), lambda b, j: (lag(b, j, 0)[0], 0, 0)),
                  pl.BlockSpec(memory_space=pl.ANY),
                  pl.BlockSpec((1, 1, d), lambda b, j: (lag(b, j, 2)[0], 0, 5)),
                  pl.BlockSpec((1, d), lambda b, j: (0, 0)),
                  pl.BlockSpec((d, ck), lambda b, j: (0, lag(b, j, 0)[1])),
                  pl.BlockSpec((d, ck), lambda b, j: (0, nj + lag(b, j, 0)[1])),
                  pl.BlockSpec((taps, ck), lambda b, j: (0, lag(b, j, 1)[1])),
                  pl.BlockSpec((taps, ck), lambda b, j: (0, nj + lag(b, j, 1)[1])),
                  pl.BlockSpec((ck, d), lambda b, j: (lag(b, j, 2)[1], 0))],
        out_specs=pl.BlockSpec((1, seq, d), lambda b, j: (lag(b, j, 2)[0], 0, 0)),
        out_shape=jax.ShapeDtypeStruct((bsz, seq, d), F32),
        scratch_shapes=([pltpu.VMEM((3, seq + 2 * GRID_W, ck), BF16)] * 4 + [pltpu.VMEM((seq, ck), BF16)] * 2
                        + [pltpu.VMEM((seq, d), F32), pltpu.SemaphoreType.DMA(())]),
        compiler_params=_params("arbitrary", "arbitrary"),
        name="ffn",
    )(h2, x1, mods3, final_g, w_up, w_up, conv_w, conv_w, w_down)


def _layer(x, ctx, mods3, ctx_row, norm1_g, norm2_g, w_in, ssm, glu_w, glu_b, sconv_w,
           proj_a, proj_b, w_out, ffn_w_up, ffn_conv_w, ffn_w_down, out_g):
    bsz, seq, d = x.shape
    ctx_len = ctx.shape[1]
    n_g, th = ssm[1].shape[0], ssm[1].shape[1]
    sw = glu_w.shape[0]
    cw = sconv_w.shape[1]
    n_h = sw // n_g
    t_chunk = th // n_h
    n_ctx, n_lat = ctx_len // t_chunk, seq // t_chunk
    tm = min(1024, seq)
    w_in_b = w_in.astype(BF16)

    u, bg, cx, ga, gb = _inproj(x, mods3, norm1_g, w_in_b, sw, cw, tm)
    u_ctx = _ctxproj(ctx, mods3, ctx_row, norm1_g, w_in_b[:, :sw], min(tm, ctx_len))
    ya = _from_chunks(_ssm(_to_chunks(u_ctx, n_g), _to_chunks(u, n_g), *ssm))

    x1, h2 = _merge(ya, bg, cx, ga, gb, x, mods3, glu_w.astype(BF16), glu_b.reshape(1, sw), sconv_w,
                    proj_a.astype(BF16), proj_b.astype(BF16), w_out.astype(BF16), norm2_g, tm)
    k2 = ffn_conv_w.shape[0] * ffn_conv_w.shape[1]
    return _ffn(h2, x1, mods3, out_g, ffn_w_up.astype(BF16), ffn_conv_w.reshape(k2, -1),
                ffn_w_down.astype(BF16), 256)


def kernel(x, c, ctx, c_ctx, mod_w, mod_b, norm1_g, norm2_g, w_in, ssm_lambda_re, ssm_lambda_im,
           ssm_log_dt, ssm_b_re, ssm_b_im, ssm_c_re, ssm_c_im, ssm_d, ssm_glu_w, ssm_glu_b,
           sconv_w, proj_a, proj_b, w_out, ffn_w_up, ffn_conv_w, ffn_w_down, final_g):
    depth = mod_w.shape[0]
    assert depth == 1, "context-stream update between layers is not implemented"
    bsz, seq, d = x.shape
    assert seq % GRID_W == 0 and seq % SSM_CHUNK == 0 and ctx.shape[1] % SSM_CHUNK == 0
    i = 0
    rows = -(-(bsz + 1) // 8) * 8
    cond = jnp.zeros((rows, d), F32).at[:bsz].set(c).at[bsz].set(c_ctx)
    mods3 = _adaln(cond, mod_w[i], mod_b[i]).reshape(rows, 1, N_MOD * d)
    ssm = _ssm_weights(ssm_lambda_re[i], ssm_lambda_im[i], ssm_log_dt[i], ssm_b_re[i], ssm_b_im[i],
                       ssm_c_re[i], ssm_c_im[i], ssm_d[i], SSM_CHUNK)
    return _layer(x, ctx, mods3, bsz, norm1_g[i].reshape(1, d), norm2_g[i].reshape(1, d), w_in[i], ssm,
                  ssm_glu_w[i], ssm_glu_b[i], sconv_w[i], proj_a[i], proj_b[i], w_out[i],
                  ffn_w_up[i], ffn_conv_w[i], ffn_w_down[i], final_g.reshape(1, d))
```
